```python
import functools
import jax, jax.numpy as jnp
from jax import lax
import numpy as np

D_MODEL = 1024
BATCH = 8
SEQ = 2048
DEPTH = 1
DEC_BATCH = 32
DEC_SEQ = 8
PAST_LEN = 16384
PAGE_SIZE = 128

MIX_WIDTH = D_MODEL
MOBA_HEADS = 8
MOBA_HEAD_DIM = 64
D_MOBA = MOBA_HEADS * MOBA_HEAD_DIM
MOBA_BLOCK = 256
MOBA_TOPK = 3
Q_BLOCK = 32
ROPE_THETA = 500000.0
ROPE_DIMS = MOBA_HEAD_DIM // 4
GLA_HEADS = 4
GLA_DK = 64
GLA_DV = 128
D_GLA_K = GLA_HEADS * GLA_DK
D_GLA = GLA_HEADS * GLA_DV
GLA_GATE_RANK = 16
GLA_GATE_NORM = 16.0
GLA_CHUNK = 64
D_FF = 2816
N_SUBLAYERS = 3
ALPHA = (2.0 * DEPTH) ** 0.25
BETA = (8.0 * DEPTH) ** -0.25
LN_EPS = 1e-5
RMS_EPS = 1e-6
MIX_COLS = (D_MOBA, D_MOBA, D_MOBA, D_GLA_K, D_GLA_K, D_GLA, D_GLA, GLA_GATE_RANK)
MIX_SPLITS = tuple(int(s) for s in np.cumsum(MIX_COLS)[:-1])
D_MIX_IN = int(sum(MIX_COLS))

kernel_name = 'hybrid_moba_gla_macaron_step'


def layer_norm(x, g, b):
    xf = x.astype(jnp.float32)
    mu = jnp.mean(xf, axis=-1, keepdims=True)
    var = jnp.mean(jnp.square(xf - mu), axis=-1, keepdims=True)
    return ((xf - mu) * lax.rsqrt(var + LN_EPS) * g + b).astype(x.dtype)


def ada_modulation(c, w_ada, b_ada):
    m = jax.nn.silu(c) @ w_ada + b_ada
    return m.reshape(c.shape[0], 3 * N_SUBLAYERS, 1, D_MODEL)


def post_norm_residual(x, out, gate, res_w, g, b):
    return layer_norm(ALPHA * x + res_w * (1.0 + gate) * out, g, b)


def swiglu(h, w_in, w_out):
    a, u = jnp.split(h @ w_in, 2, axis=-1)
    return (jax.nn.silu(a) * u) @ w_out


def rope_partial(x, pos):
    half = ROPE_DIMS // 2
    inv = ROPE_THETA ** (-jnp.arange(half, dtype=jnp.float32) / half)
    ang = pos.astype(jnp.float32)[:, None] * inv[None, :]
    cos = jnp.cos(ang)[:, None, :]
    sin = jnp.sin(ang)[:, None, :]
    xr = x[..., :ROPE_DIMS].astype(jnp.float32)
    x1, x2 = xr[..., :half], xr[..., half:]
    rot = jnp.concatenate([x1 * cos - x2 * sin, x2 * cos + x1 * sin], axis=-1).astype(x.dtype)
    return jnp.concatenate([rot, x[..., ROPE_DIMS:]], axis=-1)


def mixer_projections(h, pos, w_mix_in, w_gk_up, b_gk):
    B, S, _ = h.shape
    qm, km, vm, qg, kg, vg, gg, rg = jnp.split(h @ w_mix_in, MIX_SPLITS, axis=-1)
    qm = rope_partial(qm.reshape(B, S, MOBA_HEADS, MOBA_HEAD_DIM), pos)
    km = rope_partial(km.reshape(B, S, MOBA_HEADS, MOBA_HEAD_DIM), pos)
    vm = vm.reshape(B, S, MOBA_HEADS, MOBA_HEAD_DIM)
    qg = qg.reshape(B, S, GLA_HEADS, GLA_DK) * (GLA_DK ** -0.5)
    kg = kg.reshape(B, S, GLA_HEADS, GLA_DK)
    vg = vg.reshape(B, S, GLA_HEADS, GLA_DV)
    log_a = jax.nn.log_sigmoid((rg @ w_gk_up + b_gk).astype(jnp.float32)) / GLA_GATE_NORM
    return qm, km, vm, qg, kg, vg, gg, log_a.reshape(B, S, GLA_HEADS, GLA_DK)


def moba_select(q, kmean, n_valid):
    gate = jnp.einsum('bqhd,bnhd->bhqn', q, kmean, preferred_element_type=jnp.float32)
    gate = jnp.where(jnp.arange(kmean.shape[1]) < n_valid, gate, -jnp.inf)
    _, idx = lax.top_k(gate, MOBA_TOPK)
    return idx, idx < n_valid


def sparse_softmax_attend(s_sel, v_sel, s_own, v_own):
    n = s_sel.shape[-1]
    p = jax.nn.softmax(jnp.concatenate([s_sel, s_own], axis=-1), axis=-1)
    o = jnp.einsum('bhqn,bhqnd->bqhd', p[..., :n].astype(v_sel.dtype), v_sel)
    return o + jnp.einsum('bhqm,bmhd->bqhd', p[..., n:].astype(v_own.dtype), v_own)


def moba_prompt(q, k, v):
    B, S, H, HD = q.shape
    n_blocks = -(-S // MOBA_BLOCK)
    n_full = S // MOBA_BLOCK
    n_slot = max(n_full, MOBA_TOPK)
    pad = n_blocks * MOBA_BLOCK - S
    kp = jnp.pad(k, ((0, 0), (0, pad), (0, 0), (0, 0)))
    vp = jnp.pad(v, ((0, 0), (0, pad), (0, 0), (0, 0)))
    kb = kp.reshape(B, n_blocks, MOBA_BLOCK, H, HD)
    vb = vp.reshape(B, n_blocks, MOBA_BLOCK, H, HD)
    kmean = jnp.mean(kb[:, :n_full].astype(jnp.float32), axis=2).astype(q.dtype)
    kmean = jnp.pad(kmean, ((0, 0), (0, n_slot - n_full), (0, 0), (0, 0)))
    b_ix = jnp.arange(B)[:, None, None, None]
    h_ix = jnp.arange(H)[None, :, None, None]
    scale = HD ** -0.5

    def attend_block(i):
        q0 = i * Q_BLOCK
        own = q0 // MOBA_BLOCK
        qi = lax.dynamic_slice_in_dim(q, q0, Q_BLOCK, axis=1)
        idx, ok = moba_select(qi, kmean, own)
        idx = jnp.minimum(idx, n_blocks - 1)
        k_sel = kb[b_ix, idx, :, h_ix]
        v_sel = vb[b_ix, idx, :, h_ix]
        s_sel = jnp.einsum('bqhd,bhqnkd->bhqnk', qi, k_sel, preferred_element_type=jnp.float32) * scale
        s_sel = jnp.where(ok[..., None], s_sel, -jnp.inf).reshape(B, H, Q_BLOCK, -1)
        k0 = own * MOBA_BLOCK
        k_own = lax.dynamic_slice_in_dim(kp, k0, MOBA_BLOCK, axis=1)
        v_own = lax.dynamic_slice_in_dim(vp, k0, MOBA_BLOCK, axis=1)
        s_own = jnp.einsum('bqhd,bkhd->bhqk', qi, k_own, preferred_element_type=jnp.float32) * scale
        causal = (k0 + jnp.arange(MOBA_BLOCK))[None, :] <= (q0 + jnp.arange(Q_BLOCK))[:, None]
        s_own = jnp.where(causal, s_own, -jnp.inf)
        return sparse_softmax_attend(s_sel, v_sel.reshape(B, H, Q_BLOCK, -1, HD), s_own, v_own)

    o = lax.map(attend_block, jnp.arange(S // Q_BLOCK))
    return o.transpose(1, 0, 2, 3, 4).reshape(B, S, H * HD)


def moba_sample(q, k_new, v_new, cache_k, cache_v, page_table, layer):
    B, T, H, HD = q.shape
    n_pages = page_table.shape[1]
    past = n_pages * PAGE_SIZE
    n_full = past // MOBA_BLOCK
    own0 = n_full * MOBA_BLOCK
    n_slot = max(n_full, MOBA_TOPK)
    ppb = MOBA_BLOCK // PAGE_SIZE
    scale = HD ** -0.5
    k_past = cache_k[layer, page_table].reshape(B, past, H, HD)
    kmean = jnp.mean(k_past[:, :own0].reshape(B, n_full, MOBA_BLOCK, H, HD).astype(jnp.float32), axis=2).astype(q.dtype)
    kmean = jnp.pad(kmean, ((0, 0), (0, n_slot - n_full), (0, 0), (0, 0)))
    idx, ok = moba_select(q, kmean, n_full)
    idx = jnp.clip(idx, 0, max(n_full - 1, 0))
    lp = jnp.minimum(idx[..., None] * ppb + jnp.arange(ppb), n_pages - 1)
    phys = page_table[jnp.arange(B)[:, None, None, None, None], lp]
    rows = jnp.arange(PAGE_SIZE)
    h_ix = jnp.arange(H)[None, :, None, None, None, None]
    k_sel = cache_k[layer, phys[..., None], rows, h_ix].reshape(B, H, T, MOBA_TOPK, MOBA_BLOCK, HD)
    v_sel = cache_v[layer, phys[..., None], rows, h_ix].reshape(B, H, T, MOBA_TOPK * MOBA_BLOCK, HD)
    s_sel = jnp.einsum('bqhd,bhqnkd->bhqnk', q, k_sel, preferred_element_type=jnp.float32) * scale
    s_sel = jnp.where(ok[..., None], s_sel, -jnp.inf).reshape(B, H, T, -1)
    r = past - own0
    k_own = jnp.concatenate([k_past[:, own0:], k_new], axis=1)
    v_own_past = cache_v[layer, page_table[:, own0 // PAGE_SIZE:]].reshape(B, r, H, HD)
    v_own = jnp.concatenate([v_own_past, v_new], axis=1)
    s_own = jnp.einsum('bqhd,bkhd->bhqk', q, k_own, preferred_element_type=jnp.float32) * scale
    causal = jnp.arange(r + T)[None, :] <= (r + jnp.arange(T))[:, None]
    s_own = jnp.where(causal, s_own, -jnp.inf)
    o = sparse_softmax_attend(s_sel, v_sel, s_own, v_own)
    return o.reshape(B, T, H * HD)


def gla_chunked(q, k, v, log_a, s0, chunk):
    B, S, H, _ = q.shape
    n = S // chunk

    def to_chunks(a):
        return a.astype(jnp.float32).reshape(B, n, chunk, H, a.shape[-1]).transpose(1, 0, 3, 2, 4)

    causal = jnp.tril(jnp.ones((chunk, chunk), dtype=bool))

    def step(s, inp):
        qc, kc, vc, gc = inp
        b = jnp.cumsum(gc, axis=2)
        diff = b[:, :, :, None, :] - b[:, :, None, :, :]
        decay = jnp.exp(jnp.where(causal[:, :, None], diff, -jnp.inf))
        att = jnp.einsum('bhid,bhjd,bhijd->bhij', qc, kc, decay)
        o = jnp.einsum('bhij,bhjv->bhiv', att, vc) + jnp.einsum('bhid,bhdv->bhiv', qc * jnp.exp(b), s)
        b_end = b[:, :, -1, :]
        s = jnp.exp(b_end)[..., None] * s + jnp.einsum('bhjd,bhjv->bhdv', kc * jnp.exp(b_end[:, :, None, :] - b), vc)
        return s, o

    s_fin, o = lax.scan(step, s0.astype(jnp.float32), (to_chunks(q), to_chunks(k), to_chunks(v), to_chunks(log_a)))
    return o.transpose(1, 0, 3, 2, 4).reshape(B, S, H, -1), s_fin


def gla_output(o, gg, norm_g, dtype):
    B, S = o.shape[:2]
    o = o * lax.rsqrt(jnp.mean(jnp.square(o), axis=-1, keepdims=True) + RMS_EPS) * norm_g
    return (o.astype(dtype) * jax.nn.silu(gg.reshape(B, S, GLA_HEADS, GLA_DV))).reshape(B, S, D_GLA)


def decoder_layer(x, c, pos, moba_fn, gla_s0, gla_chunk, w_ada, b_ada, ln_g, ln_b,
                  w_ffn1_in, w_ffn1_out, w_mix_in, w_gk_up, b_gk, gla_norm_g, w_mix_out,
                  w_ffn2_in, w_ffn2_out):
    m = ada_modulation(c, w_ada, b_ada)
    h = x * (1.0 + m[:, 1]) + m[:, 0]
    x = post_norm_residual(x, swiglu(h, w_ffn1_in, w_ffn1_out), m[:, 2], 0.5, ln_g[0], ln_b[0])
    h = x * (1.0 + m[:, 4]) + m[:, 3]
    qm, km, vm, qg, kg, vg, gg, log_a = mixer_projections(h, pos, w_mix_in, w_gk_up, b_gk)
    o_moba = moba_fn(qm, km, vm)
    o_gla, s_fin = gla_chunked(qg, kg, vg, log_a, gla_s0, gla_chunk)
    o_gla = gla_output(o_gla, gg, gla_norm_g, x.dtype)
    mix = jnp.concatenate([o_moba, o_gla], axis=-1) @ w_mix_out
    x = post_norm_residual(x, mix, m[:, 5], 1.0, ln_g[1], ln_b[1])
    h = x * (1.0 + m[:, 7]) + m[:, 6]
    x = post_norm_residual(x, swiglu(h, w_ffn2_in, w_ffn2_out), m[:, 8], 0.5, ln_g[2], ln_b[2])
    return x, km, vm, s_fin


def setup_inputs(seed: int = 0) -> dict:
    key = jax.random.key(seed)
    ks = jax.random.split(key, 21)
    n_pages = PAST_LEN // PAGE_SIZE
    n_phys = (DEC_BATCH * n_pages * 5) // 4

    def nrm(k, shape, scale):
        return jax.random.normal(k, shape, jnp.float32) * scale

    page_table = jax.random.permutation(ks[5], n_phys)[:DEC_BATCH * n_pages]
    page_table = page_table.reshape(DEC_BATCH, n_pages).astype(jnp.int32)
    return {
        'x_prompt': nrm(ks[0], (BATCH, SEQ, D_MODEL), 1.0),
        'x_sample': nrm(ks[1], (DEC_BATCH, DEC_SEQ, D_MODEL), 1.0),
        'cache_k': nrm(ks[2], (DEPTH, n_phys, PAGE_SIZE, MOBA_HEADS, MOBA_HEAD_DIM), 1.0),
        'cache_v': nrm(ks[3], (DEPTH, n_phys, PAGE_SIZE, MOBA_HEADS, MOBA_HEAD_DIM), 1.0),
        'state_gla': nrm(ks[4], (DEPTH, DEC_BATCH, GLA_HEADS, GLA_DK, GLA_DV), 1.0),
        'page_table': page_table,
        'c_prompt': nrm(ks[6], (BATCH, D_MODEL), 1.0),
        'c_sample': nrm(ks[7], (DEC_BATCH, D_MODEL), 1.0),
        'w_ada': nrm(ks[8], (DEPTH, D_MODEL, 3 * N_SUBLAYERS * D_MODEL), 0.5 * D_MODEL ** -0.5),
        'b_ada': nrm(ks[9], (DEPTH, 3 * N_SUBLAYERS * D_MODEL), 0.01),
        'ln_g': 1.0 + nrm(ks[10], (DEPTH, N_SUBLAYERS, D_MODEL), 0.01),
        'ln_b': nrm(ks[11], (DEPTH, N_SUBLAYERS, D_MODEL), 0.01),
        'w_ffn1_in': nrm(ks[12], (DEPTH, D_MODEL, 2 * D_FF), D_MODEL ** -0.5),
        'w_ffn1_out': nrm(ks[13], (DEPTH, D_FF, D_MODEL), BETA * D_FF ** -0.5),
        'w_mix_in': nrm(ks[14], (DEPTH, D_MODEL, D_MIX_IN), D_MODEL ** -0.5),
        'w_gk_up': nrm(ks[15], (DEPTH, GLA_GATE_RANK, D_GLA_K), GLA_GATE_RANK ** -0.5),
        'b_gk': nrm(ks[16], (DEPTH, D_GLA_K), 0.1),
        'gla_norm_g': 1.0 + nrm(ks[17], (DEPTH, GLA_DV), 0.01),
        'w_mix_out': nrm(ks[18], (DEPTH, MIX_WIDTH, D_MODEL), BETA * MIX_WIDTH ** -0.5),
        'w_ffn2_in': nrm(ks[19], (DEPTH, D_MODEL, 2 * D_FF), D_MODEL ** -0.5),
        'w_ffn2_out': nrm(ks[20], (DEPTH, D_FF, D_MODEL), BETA * D_FF ** -0.5),
    }


def reference(x_prompt, x_sample, cache_k, cache_v, state_gla, page_table, c_prompt, c_sample,
              w_ada, b_ada, ln_g, ln_b, w_ffn1_in, w_ffn1_out, w_mix_in, w_gk_up, b_gk,
              gla_norm_g, w_mix_out, w_ffn2_in, w_ffn2_out):
    B, S, _ = x_prompt.shape
    T = x_sample.shape[1]
    past_len = page_table.shape[1] * PAGE_SIZE
    pos_p = jnp.arange(S)
    pos_s = past_len + jnp.arange(T)
    xp, xs = x_prompt, x_sample
    kp_rows, vp_rows, sp_list, ks_rows, vs_rows, ss_list = [], [], [], [], [], []
    for l in range(DEPTH):
        s0_prompt = jnp.zeros((B, GLA_HEADS, GLA_DK, GLA_DV), jnp.float32)
        xp, kp_, vp_, sp_ = decoder_layer(
            xp, c_prompt, pos_p, moba_prompt, s0_prompt, min(GLA_CHUNK, S),
            w_ada[l], b_ada[l], ln_g[l], ln_b[l], w_ffn1_in[l], w_ffn1_out[l], w_mix_in[l],
            w_gk_up[l], b_gk[l], gla_norm_g[l], w_mix_out[l], w_ffn2_in[l], w_ffn2_out[l])
        sample_attn = functools.partial(moba_sample, cache_k=cache_k, cache_v=cache_v,
                                        page_table=page_table, layer=l)
        xs, ks_, vs_, ss_ = decoder_layer(
            xs, c_sample, pos_s, sample_attn, state_gla[l], T,
            w_ada[l], b_ada[l], ln_g[l], ln_b[l], w_ffn1_in[l], w_ffn1_out[l], w_mix_in[l],
            w_gk_up[l], b_gk[l], gla_norm_g[l], w_mix_out[l], w_ffn2_in[l], w_ffn2_out[l])
        kp_rows.append(kp_)
        vp_rows.append(vp_)
        sp_list.append(sp_.astype(state_gla.dtype))
        ks_rows.append(ks_)
        vs_rows.append(vs_)
        ss_list.append(ss_.astype(state_gla.dtype))
    k_prompt = jnp.stack(kp_rows)
    v_prompt = jnp.stack(vp_rows)
    gla_prompt = jnp.stack(sp_list)
    k_sample = jnp.stack(ks_rows)
    v_sample = jnp.stack(vs_rows)
    gla_sample = jnp.stack(ss_list)
    return (xp, xs, k_prompt, v_prompt, gla_prompt, k_sample, v_sample, gla_sample)
```

```python
import functools

import jax
import jax.numpy as jnp
from jax import lax
from jax.experimental import pallas as pl
from jax.experimental.pallas import tpu as pltpu

F32 = jnp.float32
BF16 = jnp.bfloat16

PAGE_SIZE = 128
MOBA_HEADS = 8
MOBA_HEAD_DIM = 64
D_MOBA = MOBA_HEADS * MOBA_HEAD_DIM
MOBA_BLOCK = 256
MOBA_TOPK = 3
ROPE_THETA = 500000.0
ROPE_DIMS = MOBA_HEAD_DIM // 4
ROPE_HALF = ROPE_DIMS // 2
GLA_HEADS = 4
GLA_DK = 64
GLA_DV = 128
D_GLA_K = GLA_HEADS * GLA_DK
D_GLA = GLA_HEADS * GLA_DV
GLA_GATE_RANK = 16
GLA_GATE_NORM = 16.0
D_MIX_MAIN = 3 * D_MOBA + 2 * D_GLA_K + 2 * D_GLA
N_SUBLAYERS = 3
LN_EPS = 1e-5
RMS_EPS = 1e-6

LANES = 128
NEG_BIG = -1e30
VMEM_LIMIT = 56 * 1024 * 1024

FFN_ROWS = 512
GLA_CHUNK = 128
PAGES_PER_STEP = 8


def _cparams(sem):
    return pltpu.CompilerParams(dimension_semantics=sem, vmem_limit_bytes=VMEM_LIMIT)


def _dot(a, b):
    return jnp.dot(a.astype(BF16), b.astype(BF16), preferred_element_type=F32)


def _dot_nt(a, b):
    return lax.dot_general(a.astype(BF16), b.astype(BF16), (((1,), (1,)), ((), ())),
                           preferred_element_type=F32)


def _split2(x):
    hi = x.astype(BF16)
    lo = (x - hi.astype(F32)).astype(BF16)
    return hi, lo


def _dot3(a, b, nt=False):
    d = _dot_nt if nt else _dot
    ah, al = _split2(a)
    bh, bl = _split2(b)
    return d(ah, bh) + (d(ah, bl) + d(al, bh))


def _dot_exact_lhs(lhs_bf16, x):
    hi = x.astype(BF16)
    r1 = x - hi.astype(F32)
    mid = r1.astype(BF16)
    lo = (r1 - mid.astype(F32)).astype(BF16)
    f = functools.partial(jnp.dot, lhs_bf16, preferred_element_type=F32)
    return f(hi) + (f(mid) + f(lo))


def _silu(x):
    return x * jax.nn.sigmoid(x)


def _layer_norm(y, g, b):
    mu = jnp.mean(y, axis=-1, keepdims=True)
    yc = y - mu
    var = jnp.mean(yc * yc, axis=-1, keepdims=True)
    return yc * lax.rsqrt(var + LN_EPS) * g + b


def _pad_rows(x, rows):
    return jnp.concatenate([x, jnp.zeros((rows - x.shape[0], x.shape[1]), x.dtype)], axis=0)


def _ada_kernel(c_ref, w_ref, b_ref, o_ref):
    o_ref[0] = _dot3(_silu(c_ref[...]), w_ref[...]) + b_ref[...]


def _ada_modulation(c_all, w_ada, b_ada):
    nb, d = c_all.shape
    n_out = w_ada.shape[1] // d
    return pl.pallas_call(
        _ada_kernel,
        out_shape=jax.ShapeDtypeStruct((n_out, nb, d), F32),
        grid=(n_out,),
        in_specs=[pl.BlockSpec((nb, d), lambda n: (0, 0)),
                  pl.BlockSpec((d, d), lambda n: (0, n)),
                  pl.BlockSpec((1, d), lambda n: (0, n))],
        out_specs=pl.BlockSpec((1, nb, d), lambda n: (n, 0, 0)),
        compiler_params=_cparams(("arbitrary",)),
        name="ada_modulation",
    )(c_all, w_ada, b_ada.reshape(1, -1))


def _ffn_kernel(x_ref, m_ref, wa_ref, wu_ref, wo_ref, g_ref, b_ref, o_ref, h_scr, acc_scr, *, alpha):
    j = pl.program_id(1)
    bb, ts, d = x_ref.shape

    @pl.when(j == 0)
    def _():
        h = x_ref[...] * (1.0 + m_ref[1]) + m_ref[0]
        h_scr[...] = h.reshape(bb * ts, d).astype(BF16)
        acc_scr[...] = jnp.zeros_like(acc_scr)

    h = h_scr[...]
    a = jnp.dot(h, wa_ref[...], preferred_element_type=F32)
    u = jnp.dot(h, wu_ref[...], preferred_element_type=F32)
    t = (_silu(a) * u).astype(BF16)
    acc_scr[...] += jnp.dot(t, wo_ref[...], preferred_element_type=F32)

    @pl.when(j == pl.num_programs(1) - 1)
    def _():
        y = alpha * x_ref[...] + (0.5 * (1.0 + m_ref[2])) * acc_scr[...].reshape(bb, ts, d)
        o_ref[...] = _layer_norm(y, g_ref[...], b_ref[...])


def _row_tiling(x):
    nb, s, _ = x.shape
    if s >= FFN_ROWS:
        assert s % FFN_ROWS == 0
        return 1, FFN_ROWS
    assert s % 8 == 0 and FFN_ROWS % s == 0
    bb = min(nb, FFN_ROWS // s)
    assert nb % bb == 0
    return bb, s


def _ffn(x, m4, m_row0, sub, w_in, w_out, ln_g, ln_b, alpha, n_chunks=2):
    nb, s, d = x.shape
    bb, ts = _row_tiling(x)
    tpb = s // ts
    d_ff = w_out.shape[0]
    ck = d_ff // n_chunks
    assert ck * n_chunks == d_ff and ck % LANES == 0 and m_row0 % bb == 0
    grid = ((nb // bb) * tpb, n_chunks)
    xmap = lambda i, j: (i // tpb, i % tpb, 0)
    mmap = lambda i, j: (sub, m_row0 // bb + i // tpb, 0, 0)
    return pl.pallas_call(
        functools.partial(_ffn_kernel, alpha=alpha),
        out_shape=jax.ShapeDtypeStruct(x.shape, F32),
        grid=grid,
        in_specs=[pl.BlockSpec((bb, ts, d), xmap),
                  pl.BlockSpec((3, bb, 1, d), mmap),
                  pl.BlockSpec((d, ck), lambda i, j: (0, j)),
                  pl.BlockSpec((d, ck), lambda i, j: (0, j + n_chunks)),
                  pl.BlockSpec((ck, d), lambda i, j: (j, 0)),
                  pl.BlockSpec((1, d), lambda i, j: (0, 0)),
                  pl.BlockSpec((1, d), lambda i, j: (0, 0))],
        out_specs=pl.BlockSpec((bb, ts, d), xmap),
        scratch_shapes=[pltpu.VMEM((bb * ts, d), BF16), pltpu.VMEM((bb * ts, d), F32)],
        compiler_params=_cparams(("parallel", "arbitrary")),
        name="ffn_postnorm",
    )(x, m4, w_in, w_in, w_out, ln_g.reshape(1, d), ln_b.reshape(1, d))


def _mixin_kernel(x_ref, m_ref, w_ref, wrg_ref, wgk_ref, bgk_ref,
                  qm_ref, km_ref, vm_ref, qg_ref, kg_ref, vg_ref, gg_ref, la_ref,
                  *, pos_base, tiles_per_batch, kv_transposed):
    i = pl.program_id(0)
    bb, ts, d = x_ref.shape
    tm = bb * ts
    h = (x_ref[...] * (1.0 + m_ref[1]) + m_ref[0]).reshape(tm, d).astype(BF16)
    p = jnp.dot(h, w_ref[...], preferred_element_type=F32)

    lane = lax.broadcasted_iota(jnp.int32, (1, LANES), 1)
    fi = (lane & (ROPE_HALF - 1)).astype(F32)
    inv = jnp.power(jnp.full((1, LANES), ROPE_THETA, F32), -fi / ROPE_HALF)
    row = lax.broadcasted_iota(jnp.int32, (tm, 1), 0)
    pos = (pos_base + (i % tiles_per_batch) * ts + (row & (ts - 1))).astype(F32)
    ang = pos * inv
    cos = jnp.cos(ang)
    sin = jnp.sin(ang)
    l64 = lane & (MOBA_HEAD_DIM - 1)
    c_tab = jnp.where(l64 < ROPE_DIMS, cos, 1.0)
    s_lo = jnp.where(l64 < ROPE_HALF, -sin, 0.0)
    s_hi = jnp.where((l64 >= ROPE_HALF) & (l64 < ROPE_DIMS), sin, 0.0)

    def rope(x):
        slabs = []
        for s in range(x.shape[1] // LANES):
            xs = x[:, s * LANES:(s + 1) * LANES]
            slabs.append(xs * c_tab + pltpu.roll(xs, LANES - ROPE_HALF, 1) * s_lo
                         + pltpu.roll(xs, ROPE_HALF, 1) * s_hi)
        return jnp.concatenate(slabs, axis=1)

    def store(ref, x):
        ref[...] = x.reshape(ref.shape)

    def store_kv(ref, x):
        if kv_transposed:
            ref[0] = x.T
        else:
            store(ref, x)

    o = 0
    store(qm_ref, rope(p[:, o:o + D_MOBA])); o += D_MOBA
    store_kv(km_ref, rope(p[:, o:o + D_MOBA])); o += D_MOBA
    store_kv(vm_ref, p[:, o:o + D_MOBA]); o += D_MOBA
    store(qg_ref, p[:, o:o + D_GLA_K] * (GLA_DK ** -0.5)); o += D_GLA_K
    store(kg_ref, p[:, o:o + D_GLA_K]); o += D_GLA_K
    store(vg_ref, p[:, o:o + D_GLA]); o += D_GLA
    store(gg_ref, p[:, o:o + D_GLA]); o += D_GLA

    rg = jnp.dot(h, wrg_ref[...], preferred_element_type=F32)
    z = _dot3(rg, wgk_ref[...]) + bgk_ref[...]
    log_sig = jnp.minimum(z, 0.0) - jnp.log1p(jnp.exp(-jnp.abs(z)))
    store(la_ref, log_sig / GLA_GATE_NORM)


def _mixer_in(x, m4, m_row0, w_main, w_rg, w_gk, b_gk, pos_base, kv_transposed):
    nb, s, d = x.shape
    bb, ts = _row_tiling(x)
    tpb = s // ts
    assert m_row0 % bb == 0 and (bb == 1 or not kv_transposed)
    grid = ((nb // bb) * tpb,)
    xmap = lambda i: (i // tpb, i % tpb, 0)
    tmap = lambda i: (i // tpb, 0, i % tpb)
    mmap = lambda i: (1, m_row0 // bb + i // tpb, 0, 0)
    const2 = lambda i: (0, 0)
    widths = (D_MOBA, D_MOBA, D_MOBA, D_GLA_K, D_GLA_K, D_GLA, D_GLA, D_GLA_K)
    out_shape = [jax.ShapeDtypeStruct((nb, s, w), F32) for w in widths]
    out_specs = [pl.BlockSpec((bb, ts, w), xmap) for w in widths]
    if kv_transposed:
        for n in (1, 2):
            out_shape[n] = jax.ShapeDtypeStruct((nb, D_MOBA, s), F32)
            out_specs[n] = pl.BlockSpec((1, D_MOBA, ts), tmap)
    return pl.pallas_call(
        functools.partial(_mixin_kernel, pos_base=pos_base, tiles_per_batch=tpb,
                          kv_transposed=kv_transposed),
        out_shape=out_shape,
        grid=grid,
        in_specs=[pl.BlockSpec((bb, ts, d), xmap),
                  pl.BlockSpec((3, bb, 1, d), mmap),
                  pl.BlockSpec(w_main.shape, const2),
                  pl.BlockSpec(w_rg.shape, const2),
                  pl.BlockSpec(w_gk.shape, const2),
                  pl.BlockSpec((1, D_GLA_K), const2)],
        out_specs=out_specs,
        compiler_params=_cparams(("parallel",)),
        name="mixer_in",
    )(x, m4, w_main, w_rg, w_gk, b_gk.reshape(1, -1))


def _softmax_step(s, pmask, pv, m_scr, l_scr, acc_scr):
    m_old = m_scr[...]
    m_new = jnp.maximum(m_old, jnp.max(jnp.where(pmask, s, NEG_BIG), axis=1, keepdims=True))
    p = jnp.where(pmask, jnp.exp(s - m_new), 0.0)
    alpha = jnp.exp(m_old - m_new)
    l_scr[...] = alpha * l_scr[...] + jnp.sum(p, axis=1, keepdims=True)
    acc_scr[...] = alpha * acc_scr[...] + pv(p)
    m_scr[...] = m_new


def _moba_prompt_kernel(q_ref, kt_ref, vt_ref, o_ref, kmean_scr, m_scr, l_scr, acc_scr):
    i = pl.program_id(2)
    blk = q_ref.shape[1]
    n_blocks = kt_ref.shape[2] // blk
    hd = MOBA_HEAD_DIM
    lane = lax.broadcasted_iota(jnp.int32, (1, LANES), 1)

    @pl.when(i == 0)
    def _():
        for h in range(2):
            km = jnp.zeros((hd, LANES), F32)
            for n in range(n_blocks):
                cm = jnp.mean(kt_ref[0, h * hd:(h + 1) * hd, n * blk:(n + 1) * blk], axis=1, keepdims=True)
                km = jnp.where(lane == n, cm, km)
            kmean_scr[h] = km

    q = q_ref[0]
    row = lax.broadcasted_iota(jnp.int32, (blk, 1), 0)
    col = lax.broadcasted_iota(jnp.int32, (1, blk), 1)
    causal = jnp.where(col <= row, 1.0, 0.0)
    outs = []
    for h in range(2):
        qh = q[:, h * hd:(h + 1) * hd]
        g = jnp.where(lane < i, _dot3(qh, kmean_scr[h]), -jnp.inf)
        rank = jnp.zeros(g.shape, F32)
        for r in range(1, n_blocks):
            rank += jnp.where(pltpu.roll(g, r, 1) >= g, 1.0, 0.0)
            rank += jnp.where(pltpu.roll(g, LANES - r, 1) > g, 1.0, 0.0)
        sel = jnp.where((lane < i) & (rank < MOBA_TOPK), 1.0, 0.0)

        qs = (qh * (hd ** -0.5)).astype(BF16)
        m_h, l_h, acc_h = m_scr.at[h], l_scr.at[h], acc_scr.at[h]
        m_h[...] = jnp.full(m_h.shape, NEG_BIG, F32)
        l_h[...] = jnp.zeros(l_h.shape, F32)
        acc_h[...] = jnp.zeros(acc_h.shape, F32)

        for j in range(n_blocks):
            @pl.when(j <= i)
            def _(j=j):
                kj = kt_ref[0, h * hd:(h + 1) * hd, j * blk:(j + 1) * blk]
                vj = vt_ref[0, h * hd:(h + 1) * hd, j * blk:(j + 1) * blk]
                picked = jnp.broadcast_to(sel[:, j:j + 1], (blk, blk))
                pmask = jnp.where(i == j, causal, picked) > 0.5
                _softmax_step(_dot(qs, kj), pmask, lambda p: _dot_nt(p, vj), m_h, l_h, acc_h)

        outs.append(acc_h[...] / l_h[...])
    o_ref[0] = jnp.concatenate(outs, axis=1)


def _moba_prompt(qm, kt, vt):
    nb, s, _ = qm.shape
    blk = MOBA_BLOCK
    assert s % blk == 0 and s // blk <= LANES
    n_pairs = D_MOBA // LANES
    qmap = lambda b, hp, i: (b, i, hp)
    kmap = lambda b, hp, i: (b, hp, 0)
    return pl.pallas_call(
        _moba_prompt_kernel,
        out_shape=jax.ShapeDtypeStruct((nb, s, D_MOBA), F32),
        grid=(nb, n_pairs, s // blk),
        in_specs=[pl.BlockSpec((1, blk, LANES), qmap),
                  pl.BlockSpec((1, LANES, s), kmap),
                  pl.BlockSpec((1, LANES, s), kmap)],
        out_specs=pl.BlockSpec((1, blk, LANES), qmap),
        scratch_shapes=[pltpu.VMEM((2, MOBA_HEAD_DIM, LANES), F32),
                        pltpu.VMEM((2, blk, 1), F32), pltpu.VMEM((2, blk, 1), F32),
                        pltpu.VMEM((2, blk, MOBA_HEAD_DIM), F32)],
        compiler_params=_cparams(("parallel", "parallel", "arbitrary")),
        name="moba_prompt",
    )(qm, kt, vt)


def _gla_out(o, gg, ng):
    ms = jnp.mean(o * o, axis=1, keepdims=True)
    return o * lax.rsqrt(ms + RMS_EPS) * ng * _silu(gg)


def _gla_prompt_kernel(q_ref, k_ref, la_ref, v_ref, gg_ref, ng_ref, o_ref, s_ref, st_scr):
    c = GLA_CHUNK
    n_chunks = q_ref.shape[1] // c
    lane = lax.broadcasted_iota(jnp.int32, (1, LANES), 1)
    row = lax.broadcasted_iota(jnp.int32, (c, 1), 0)
    col = lax.broadcasted_iota(jnp.int32, (1, c), 1)
    tril = col <= row
    tril_bf = jnp.where(tril, 1.0, 0.0).astype(BF16)
    ng = ng_ref[...]
    st_scr[...] = jnp.zeros_like(st_scr)

    def body(ci, carry):
        r0 = pl.multiple_of(ci * c, c)
        q = q_ref[0, pl.ds(r0, c), :]
        k = k_ref[0, pl.ds(r0, c), :]
        b = _dot_exact_lhs(tril_bf, la_ref[0, pl.ds(r0, c), :])
        b_end = b[c - 1:c, :]
        q_dec = q * jnp.exp(b)
        k_inv = k * jnp.exp(-b)
        k_dec = k * jnp.exp(b_end - b)
        e_end = jnp.exp(b_end)
        for h in range(2):
            hm = (lane >= GLA_DK * h) & (lane < GLA_DK * (h + 1))
            qb = jnp.where(hm, q_dec, 0.0)
            vh = v_ref[0, pl.ds(r0, c), h * GLA_DV:(h + 1) * GLA_DV]
            att = jnp.where(tril, _dot_nt(qb, k_inv), 0.0)
            st = st_scr[h]
            o = _dot(att, vh) + _dot_nt(qb, st)
            st_scr[h] = st * e_end + _dot(vh.T, jnp.where(hm, k_dec, 0.0))
            gg = gg_ref[0, pl.ds(r0, c), h * GLA_DV:(h + 1) * GLA_DV]
            o_ref[0, pl.ds(r0, c), h * GLA_DV:(h + 1) * GLA_DV] = _gla_out(o, gg, ng)
        return carry

    lax.fori_loop(0, n_chunks, body, 0)
    for h in range(2):
        s_ref[0, 0, h] = st_scr[h].T[h * GLA_DK:(h + 1) * GLA_DK, :]


def _gla_prompt(qg, kg, la, vg, gg, norm_g):
    nb, s, _ = qg.shape
    assert s % GLA_CHUNK == 0
    n_pairs = GLA_HEADS // 2
    map3 = lambda b, hp: (b, 0, hp)
    return pl.pallas_call(
        _gla_prompt_kernel,
        out_shape=[jax.ShapeDtypeStruct((nb, s, D_GLA), F32),
                   jax.ShapeDtypeStruct((1, nb, GLA_HEADS, GLA_DK, GLA_DV), F32)],
        grid=(nb, n_pairs),
        in_specs=[pl.BlockSpec((1, s, LANES), map3),
                  pl.BlockSpec((1, s, LANES), map3),
                  pl.BlockSpec((1, s, LANES), map3),
                  pl.BlockSpec((1, s, 2 * GLA_DV), map3),
                  pl.BlockSpec((1, s, 2 * GLA_DV), map3),
                  pl.BlockSpec((1, GLA_DV), lambda b, hp: (0, 0))],
        out_specs=[pl.BlockSpec((1, s, 2 * GLA_DV), map3),
                   pl.BlockSpec((1, 1, 2, GLA_DK, GLA_DV), lambda b, hp: (0, b, hp, 0, 0))],
        scratch_shapes=[pltpu.VMEM((2, GLA_DV, LANES), F32)],
        compiler_params=_cparams(("parallel", "parallel")),
        name="gla_prompt",
    )(qg, kg, la, vg, gg, norm_g.reshape(1, GLA_DV))


def _gla_sample_kernel(q_ref, k_ref, la_ref, v_ref, gg_ref, s0_ref, ng_ref, o_ref, s_ref):
    t = q_ref.shape[1]
    lane = lax.broadcasted_iota(jnp.int32, (1, LANES), 1)
    row = lax.broadcasted_iota(jnp.int32, (t, 1), 0)
    ng = ng_ref[...]
    eye = (lax.broadcasted_iota(jnp.int32, (GLA_DK, 1), 0)
           == lax.broadcasted_iota(jnp.int32, (1, GLA_DK), 1))
    zeros_half = jnp.zeros((GLA_DK, GLA_DV), F32)
    for hp in range(GLA_HEADS // 2):
        sl = slice(hp * LANES, (hp + 1) * LANES)
        q = q_ref[0, :, sl]
        k = k_ref[0, :, sl]
        b = la_ref[0, :, sl]
        sh = 1
        while sh < t:
            b = b + jnp.where(row >= sh, pltpu.roll(b, sh, 0), 0.0)
            sh *= 2
        b_end = b[t - 1:t, :]
        e_end = jnp.exp(b_end)
        q_dec = q * jnp.exp(b)
        k_inv = _pad_rows(k * jnp.exp(-b), LANES)
        k_dec = k * jnp.exp(b_end - b)
        for hh in range(2):
            h = 2 * hp + hh
            hm = (lane >= GLA_DK * hh) & (lane < GLA_DK * (hh + 1))
            qb = jnp.where(hm, q_dec, 0.0)
            vh = _pad_rows(v_ref[0, :, h * GLA_DV:(h + 1) * GLA_DV], LANES)
            s0 = s0_ref[0, h]
            s0_pad = jnp.concatenate([s0, zeros_half] if hh == 0 else [zeros_half, s0], axis=0)
            att = jnp.where(lane <= row, _dot_nt(qb, k_inv), 0.0)
            o = _dot(att, vh) + _dot(qb, s0_pad)
            gg = gg_ref[0, :, h * GLA_DV:(h + 1) * GLA_DV]
            o_ref[0, :, h * GLA_DV:(h + 1) * GLA_DV] = _gla_out(o, gg, ng)
            kd = _pad_rows(jnp.where(hm, k_dec, 0.0), LANES)
            upd = _dot(kd.T, vh)[hh * GLA_DK:(hh + 1) * GLA_DK, :]
            e_h = e_end[:, hh * GLA_DK:(hh + 1) * GLA_DK]
            diag = jnp.where(eye, jnp.broadcast_to(e_h, (GLA_DK, GLA_DK)), 0.0)
            s_ref[0, 0, h] = _dot3(diag, s0) + upd


def _gla_sample(qg, kg, la, vg, gg, s0, norm_g):
    nb, t, _ = qg.shape
    map3 = lambda b: (b, 0, 0)
    return pl.pallas_call(
        _gla_sample_kernel,
        out_shape=[jax.ShapeDtypeStruct((nb, t, D_GLA), F32),
                   jax.ShapeDtypeStruct((1, nb, GLA_HEADS, GLA_DK, GLA_DV), F32)],
        grid=(nb,),
        in_specs=[pl.BlockSpec((1, t, D_GLA_K), map3),
                  pl.BlockSpec((1, t, D_GLA_K), map3),
                  pl.BlockSpec((1, t, D_GLA_K), map3),
                  pl.BlockSpec((1, t, D_GLA), map3),
                  pl.BlockSpec((1, t, D_GLA), map3),
                  pl.BlockSpec((1, GLA_HEADS, GLA_DK, GLA_DV), lambda b: (b, 0, 0, 0)),
                  pl.BlockSpec((1, GLA_DV), lambda b: (0, 0))],
        out_specs=[pl.BlockSpec((1, t, D_GLA), map3),
                   pl.BlockSpec((1, 1, GLA_HEADS, GLA_DK, GLA_DV), lambda b: (0, b, 0, 0, 0))],
        compiler_params=_cparams(("parallel",)),
        name="gla_sample",
    )(qg, kg, la, vg, gg, s0, norm_g.reshape(1, GLA_DV))


def _block_diag_queries(q):
    lane = lax.broadcasted_iota(jnp.int32, (1, D_MOBA), 1)
    parts = [jnp.where((lane >= MOBA_HEAD_DIM * h) & (lane < MOBA_HEAD_DIM * (h + 1)), q, 0.0)
             for h in range(MOBA_HEADS)]
    return jnp.concatenate(parts, axis=0)


def _moba_scores_kernel(pt_ref, q_ref, *refs):
    del pt_ref
    n = PAGES_PER_STEP
    pages, s_ref, g_ref = refs[:n], refs[n], refs[n + 1]
    qbd = (_block_diag_queries(q_ref[0]) * (MOBA_HEAD_DIM ** -0.5)).astype(BF16)
    lane = lax.broadcasted_iota(jnp.int32, (1, LANES), 1)
    for p in range(n):
        s_ref[0, :, p * PAGE_SIZE:(p + 1) * PAGE_SIZE] = _dot(qbd, pages[p][0])
    g = jnp.zeros((qbd.shape[0], LANES), F32)
    for j in range(n * PAGE_SIZE // MOBA_BLOCK):
        bs = jnp.sum(s_ref[0, :, j * MOBA_BLOCK:(j + 1) * MOBA_BLOCK], axis=1, keepdims=True)
        g = jnp.where(lane == j, bs, g)
    g_ref[0, 0] = g


def _moba_attend_kernel(pt_ref, q_ref, g_ref, sc_ref, kn_ref, vn_ref, *refs, n_full):
    del pt_ref
    n = PAGES_PER_STEP
    pages, o_ref = refs[:n], refs[n]
    sel_scr, m_scr, l_scr, acc_scr = refs[n + 1:]
    s = pl.program_id(1)
    t = q_ref.shape[1]
    rows = MOBA_HEADS * t
    keys = n * PAGE_SIZE
    bps = keys // MOBA_BLOCK
    lane = lax.broadcasted_iota(jnp.int32, (1, LANES), 1)
    lane_f = lane.astype(F32)

    @pl.when(s == 0)
    def _():
        g = jnp.zeros((rows, LANES), F32)
        for st in range(g_ref.shape[1]):
            g = g + pltpu.roll(g_ref[0, st], st * bps, 1)
        g = jnp.where(lane < n_full, g, -jnp.inf)
        sel = jnp.zeros((rows, LANES), F32)
        for _ in range(MOBA_TOPK):
            mx = jnp.max(g, axis=1, keepdims=True)
            first = jnp.min(jnp.where(g == mx, lane_f, float(LANES)), axis=1, keepdims=True)
            pick = lane_f == first
            sel = jnp.where(pick, 1.0, sel)
            g = jnp.where(pick, -jnp.inf, g)
        sel_scr[...] = sel
        m_scr[...] = jnp.full(m_scr.shape, NEG_BIG, F32)
        l_scr[...] = jnp.zeros_like(l_scr)
        acc_scr[...] = jnp.zeros_like(acc_scr)

    col = lax.broadcasted_iota(jnp.int32, (1, keys), 1)
    blk_of_col = lax.shift_right_logical(col, MOBA_BLOCK.bit_length() - 1) + s * bps
    expand = jnp.where(lax.broadcasted_iota(jnp.int32, (LANES, 1), 0) == blk_of_col, 1.0, 0.0)
    pmask = _dot(sel_scr[...], expand) > 0.5

    def pv(p):
        p = p.astype(BF16)
        out = jnp.zeros(acc_scr.shape, F32)
        for pg in range(n):
            out += _dot_nt(p[:, pg * PAGE_SIZE:(pg + 1) * PAGE_SIZE], pages[pg][0])
        return out

    _softmax_step(sc_ref[0], pmask, pv, m_scr, l_scr, acc_scr)

    @pl.when(s == pl.num_programs(1) - 1)
    def _():
        qbd = _block_diag_queries(q_ref[0]) * (MOBA_HEAD_DIM ** -0.5)
        row_t = lax.broadcasted_iota(jnp.int32, (rows, 1), 0) & (t - 1)
        s_own = _dot_nt(qbd, _pad_rows(kn_ref[0], LANES))
        v_own = _pad_rows(vn_ref[0], LANES)
        _softmax_step(s_own, lane <= row_t, lambda p: _dot(p, v_own), m_scr, l_scr, acc_scr)
        out = acc_scr[...] / l_scr[...]
        lane_w = lax.broadcasted_iota(jnp.int32, (1, D_MOBA), 1)
        o = jnp.zeros((t, D_MOBA), F32)
        for h in range(MOBA_HEADS):
            hm = (lane_w >= MOBA_HEAD_DIM * h) & (lane_w < MOBA_HEAD_DIM * (h + 1))
            o = o + jnp.where(hm, out[h * t:(h + 1) * t, :], 0.0)
        o_ref[0] = o


def _moba_sample(qm, km, vm, cache_k, cache_v, page_table):
    nb, t, _ = qm.shape
    n_pages = page_table.shape[1]
    past = n_pages * PAGE_SIZE
    n_full = past // MOBA_BLOCK
    assert past % MOBA_BLOCK == 0 and MOBA_TOPK <= n_full <= LANES and t & (t - 1) == 0
    n = PAGES_PER_STEP
    assert n_pages % n == 0 and (n * PAGE_SIZE) % MOBA_BLOCK == 0
    steps = n_pages // n
    keys = n * PAGE_SIZE
    rows = MOBA_HEADS * t
    kt = jnp.transpose(cache_k, (0, 2, 3, 1)).reshape(cache_k.shape[0], D_MOBA, PAGE_SIZE)
    vt = jnp.transpose(cache_v, (0, 2, 3, 1)).reshape(cache_v.shape[0], D_MOBA, PAGE_SIZE)

    def page_spec(p):
        return pl.BlockSpec((1, D_MOBA, PAGE_SIZE), lambda b, s, pt, p=p: (pt[b, s * n + p], 0, 0))

    qspec = pl.BlockSpec((1, t, D_MOBA), lambda b, s, pt: (b, 0, 0))
    scores, gates = pl.pallas_call(
        _moba_scores_kernel,
        out_shape=[jax.ShapeDtypeStruct((nb, rows, past), F32),
                   jax.ShapeDtypeStruct((nb, steps, rows, LANES), F32)],
        grid_spec=pltpu.PrefetchScalarGridSpec(
            num_scalar_prefetch=1, grid=(nb, steps),
            in_specs=[qspec] + [page_spec(p) for p in range(n)],
            out_specs=[pl.BlockSpec((1, rows, keys), lambda b, s, pt: (b, 0, s)),
                       pl.BlockSpec((1, 1, rows, LANES), lambda b, s, pt: (b, s, 0, 0))]),
        compiler_params=_cparams(("parallel", "arbitrary")),
        name="moba_sample_scores",
    )(page_table, qm, *([kt] * n))

    return pl.pallas_call(
        functools.partial(_moba_attend_kernel, n_full=n_full),
        out_shape=jax.ShapeDtypeStruct((nb, t, D_MOBA), F32),
        grid_spec=pltpu.PrefetchScalarGridSpec(
            num_scalar_prefetch=1, grid=(nb, steps),
            in_specs=[qspec,
                      pl.BlockSpec((1, steps, rows, LANES), lambda b, s, pt: (b, 0, 0, 0)),
                      pl.BlockSpec((1, rows, keys), lambda b, s, pt: (b, 0, s)),
                      qspec, qspec] + [page_spec(p) for p in range(n)],
            out_specs=qspec,
            scratch_shapes=[pltpu.VMEM((rows, LANES), F32), pltpu.VMEM((rows, 1), F32),
                            pltpu.VMEM((rows, 1), F32), pltpu.VMEM((rows, D_MOBA), F32)]),
        compiler_params=_cparams(("parallel", "arbitrary")),
        name="moba_sample_attend",
    )(page_table, qm, gates, scores, km, vm, *([vt] * n))


def _mixout_kernel(x_ref, om_ref, og_ref, m_ref, w_ref, g_ref, b_ref, o_ref, *, alpha):
    bb, ts, d = x_ref.shape
    tm = bb * ts
    mix = (_dot(om_ref[...].reshape(tm, D_MOBA), w_ref[0:D_MOBA, :])
           + _dot(og_ref[...].reshape(tm, D_GLA), w_ref[D_MOBA:D_MOBA + D_GLA, :]))
    y = alpha * x_ref[...] + (1.0 + m_ref[2]) * mix.reshape(bb, ts, d)
    o_ref[...] = _layer_norm(y, g_ref[...], b_ref[...])


def _mixer_out(x, o_moba, o_gla, m4, m_row0, w_out, ln_g, ln_b, alpha):
    nb, s, d = x.shape
    bb, ts = _row_tiling(x)
    tpb = s // ts
    assert m_row0 % bb == 0
    xmap = lambda i: (i // tpb, i % tpb, 0)
    const2 = lambda i: (0, 0)
    return pl.pallas_call(
        functools.partial(_mixout_kernel, alpha=alpha),
        out_shape=jax.ShapeDtypeStruct(x.shape, F32),
        grid=((nb // bb) * tpb,),
        in_specs=[pl.BlockSpec((bb, ts, d), xmap),
                  pl.BlockSpec((bb, ts, D_MOBA), xmap),
                  pl.BlockSpec((bb, ts, D_GLA), xmap),
                  pl.BlockSpec((3, bb, 1, d), lambda i: (1, m_row0 // bb + i // tpb, 0, 0)),
                  pl.BlockSpec(w_out.shape, const2),
                  pl.BlockSpec((1, d), const2),
                  pl.BlockSpec((1, d), const2)],
        out_specs=pl.BlockSpec((bb, ts, d), xmap),
        compiler_params=_cparams(("parallel",)),
        name="mixer_out_postnorm",
    )(x, o_moba, o_gla, m4, w_out, ln_g.reshape(1, d), ln_b.reshape(1, d))


def kernel(x_prompt, x_sample, cache_k, cache_v, state_gla, page_table, c_prompt, c_sample, w_ada, b_ada, ln_g, ln_b, w_ffn1_in, w_ffn1_out, w_mix_in, w_gk_up, b_gk, gla_norm_g, w_mix_out, w_ffn2_in, w_ffn2_out):
    depth = w_ada.shape[0]
    assert depth == 1, "one decoder layer"
    d = x_prompt.shape[-1]
    alpha = (2.0 * depth) ** 0.25
    nb_p, s_p, _ = x_prompt.shape
    nb_s, t_s, _ = x_sample.shape
    past_len = page_table.shape[1] * PAGE_SIZE

    w1_in, w1_out = w_ffn1_in[0].astype(BF16), w_ffn1_out[0].astype(BF16)
    w2_in, w2_out = w_ffn2_in[0].astype(BF16), w_ffn2_out[0].astype(BF16)
    w_mix = w_mix_in[0]
    w_main = w_mix[:, :D_MIX_MAIN].astype(BF16)
    w_rg = jnp.pad(w_mix[:, D_MIX_MAIN:], ((0, 0), (0, LANES - GLA_GATE_RANK))).astype(BF16)
    w_gk = jnp.pad(w_gk_up[0], ((0, LANES - GLA_GATE_RANK), (0, 0)))
    w_mo = w_mix_out[0].astype(BF16)

    c_all = jnp.concatenate([c_sample, c_prompt], axis=0)
    m = _ada_modulation(c_all, w_ada[0], b_ada[0])
    m4 = m.reshape(3 * N_SUBLAYERS, c_all.shape[0], 1, d)

    def layer(x, m_row0, pos_base, kv_transposed, moba_fn, gla_fn):
        x = _ffn(x, m4, m_row0, 0, w1_in, w1_out, ln_g[0, 0], ln_b[0, 0], alpha)
        qm, km, vm, qg, kg, vg, gg, la = _mixer_in(x, m4, m_row0, w_main, w_rg, w_gk, b_gk[0],
                                                    pos_base, kv_transposed)
        o_moba = moba_fn(qm, km, vm)
        o_gla, s_fin = gla_fn(qg, kg, la, vg, gg)
        x = _mixer_out(x, o_moba, o_gla, m4, m_row0, w_mo, ln_g[0, 1], ln_b[0, 1], alpha)
        x = _ffn(x, m4, m_row0, 2, w2_in, w2_out, ln_g[0, 2], ln_b[0, 2], alpha)
        return x, km, vm, s_fin

    yp, kp_t, vp_t, sp = layer(
        x_prompt, nb_s, 0, True, _moba_prompt,
        lambda qg, kg, la, vg, gg: _gla_prompt(qg, kg, la, vg, gg, gla_norm_g[0]))
    ys, ks, vs, ss = layer(
        x_sample, 0, past_len, False,
        lambda qm, km, vm: _moba_sample(qm, km, vm, cache_k[0], cache_v[0], page_table),
        lambda qg, kg, la, vg, gg: _gla_sample(qg, kg, la, vg, gg, state_gla[0], gla_norm_g[0]))

    def rows_major(a_t):
        a = a_t.reshape(nb_p, MOBA_HEADS, MOBA_HEAD_DIM, s_p)
        return jnp.transpose(a, (0, 3, 1, 2))[None]

    def heads(a):
        return a.reshape(1, nb_s, t_s, MOBA_HEADS, MOBA_HEAD_DIM)

    return (yp, ys, rows_major(kp_t), rows_major(vp_t), sp, heads(ks), heads(vs), ss)
```

```python
import functools

import jax
import jax.numpy as jnp
from jax import lax
from jax.experimental import pallas as pl
from jax.experimental.pallas import tpu as pltpu

F32 = jnp.float32
BF16 = jnp.bfloat16

PAGE_SIZE = 128
MOBA_HEADS = 8
MOBA_HEAD_DIM = 64
D_MOBA = MOBA_HEADS * MOBA_HEAD_DIM
MOBA_BLOCK = 256
MOBA_TOPK = 3
ROPE_THETA = 500000.0
ROPE_DIMS = MOBA_HEAD_DIM // 4
ROPE_HALF = ROPE_DIMS // 2
GLA_HEADS = 4
GLA_DK = 64
GLA_DV = 128
D_GLA_K = GLA_HEADS * GLA_DK
D_GLA = GLA_HEADS * GLA_DV
GLA_GATE_RANK = 16
GLA_GATE_NORM = 16.0
D_MIX_MAIN = 3 * D_MOBA + 2 * D_GLA_K + 2 * D_GLA
N_SUBLAYERS = 3
LN_EPS = 1e-5
RMS_EPS = 1e-6

LANES = 128
NEG_BIG = -1e30
VMEM_LIMIT = 56 * 1024 * 1024

FFN_ROWS = 512
GLA_CHUNK = 128
PAGES_PER_STEP = 16


def _cparams(sem):
    return pltpu.CompilerParams(dimension_semantics=sem, vmem_limit_bytes=VMEM_LIMIT)


def _dot(a, b):
    return jnp.dot(a.astype(BF16), b.astype(BF16), preferred_element_type=F32)


def _dot_nt(a, b):
    return lax.dot_general(a.astype(BF16), b.astype(BF16), (((1,), (1,)), ((), ())),
                           preferred_element_type=F32)


def _split2(x):
    hi = x.astype(BF16)
    lo = (x - hi.astype(F32)).astype(BF16)
    return hi, lo


def _dot3(a, b, nt=False):
    d = _dot_nt if nt else _dot
    ah, al = _split2(a)
    bh, bl = _split2(b)
    return d(ah, bh) + (d(ah, bl) + d(al, bh))


def _dot_exact_lhs(lhs_bf16, x):
    hi = x.astype(BF16)
    r1 = x - hi.astype(F32)
    mid = r1.astype(BF16)
    lo = (r1 - mid.astype(F32)).astype(BF16)
    f = functools.partial(jnp.dot, lhs_bf16, preferred_element_type=F32)
    return f(hi) + (f(mid) + f(lo))


def _silu(x):
    return x * jax.nn.sigmoid(x)


def _layer_norm(y, g, b):
    mu = jnp.mean(y, axis=-1, keepdims=True)
    yc = y - mu
    var = jnp.mean(yc * yc, axis=-1, keepdims=True)
    return yc * lax.rsqrt(var + LN_EPS) * g + b


def _pad_rows(x, rows):
    return jnp.concatenate([x, jnp.zeros((rows - x.shape[0], x.shape[1]), x.dtype)], axis=0)


def _ada_kernel(c_ref, w_ref, b_ref, o_ref):
    o_ref[0] = _dot3(_silu(c_ref[...]), w_ref[...]) + b_ref[...]


def _ada_modulation(c_all, w_ada, b_ada):
    nb, d = c_all.shape
    n_out = w_ada.shape[1] // d
    return pl.pallas_call(
        _ada_kernel,
        out_shape=jax.ShapeDtypeStruct((n_out, nb, d), F32),
        grid=(n_out,),
        in_specs=[pl.BlockSpec((nb, d), lambda n: (0, 0)),
                  pl.BlockSpec((d, d), lambda n: (0, n)),
                  pl.BlockSpec((1, d), lambda n: (0, n))],
        out_specs=pl.BlockSpec((1, nb, d), lambda n: (n, 0, 0)),
        compiler_params=_cparams(("arbitrary",)),
        name="ada_modulation",
    )(c_all, w_ada, b_ada.reshape(1, -1))


def _ffn_kernel(x_ref, m_ref, wa_ref, wu_ref, wo_ref, g_ref, b_ref, o_ref, h_scr, acc_scr, *, alpha):
    j = pl.program_id(1)
    bb, ts, d = x_ref.shape

    @pl.when(j == 0)
    def _():
        h = x_ref[...] * (1.0 + m_ref[1]) + m_ref[0]
        h_scr[...] = h.reshape(bb * ts, d).astype(BF16)
        acc_scr[...] = jnp.zeros_like(acc_scr)

    h = h_scr[...]
    a = jnp.dot(h, wa_ref[...], preferred_element_type=F32)
    u = jnp.dot(h, wu_ref[...], preferred_element_type=F32)
    t = (_silu(a) * u).astype(BF16)
    acc_scr[...] += jnp.dot(t, wo_ref[...], preferred_element_type=F32)

    @pl.when(j == pl.num_programs(1) - 1)
    def _():
        y = alpha * x_ref[...] + (0.5 * (1.0 + m_ref[2])) * acc_scr[...].reshape(bb, ts, d)
        o_ref[...] = _layer_norm(y, g_ref[...], b_ref[...])


def _row_tiling(x):
    nb, s, _ = x.shape
    if s >= FFN_ROWS:
        assert s % FFN_ROWS == 0
        return 1, FFN_ROWS
    assert s % 8 == 0 and FFN_ROWS % s == 0
    bb = min(nb, FFN_ROWS // s)
    assert nb % bb == 0
    return bb, s


def _ffn(x, m4, m_row0, sub, w_in, w_out, ln_g, ln_b, alpha, n_chunks=2):
    nb, s, d = x.shape
    bb, ts = _row_tiling(x)
    tpb = s // ts
    d_ff = w_out.shape[0]
    ck = d_ff // n_chunks
    assert ck * n_chunks == d_ff and ck % LANES == 0 and m_row0 % bb == 0
    grid = ((nb // bb) * tpb, n_chunks)
    xmap = lambda i, j: (i // tpb, i % tpb, 0)
    mmap = lambda i, j: (sub, m_row0 // bb + i // tpb, 0, 0)
    return pl.pallas_call(
        functools.partial(_ffn_kernel, alpha=alpha),
        out_shape=jax.ShapeDtypeStruct(x.shape, F32),
        grid=grid,
        in_specs=[pl.BlockSpec((bb, ts, d), xmap),
                  pl.BlockSpec((3, bb, 1, d), mmap),
                  pl.BlockSpec((d, ck), lambda i, j: (0, j)),
                  pl.BlockSpec((d, ck), lambda i, j: (0, j + n_chunks)),
                  pl.BlockSpec((ck, d), lambda i, j: (j, 0)),
                  pl.BlockSpec((1, d), lambda i, j: (0, 0)),
                  pl.BlockSpec((1, d), lambda i, j: (0, 0))],
        out_specs=pl.BlockSpec((bb, ts, d), xmap),
        scratch_shapes=[pltpu.VMEM((bb * ts, d), BF16), pltpu.VMEM((bb * ts, d), F32)],
        compiler_params=_cparams(("parallel", "arbitrary")),
        name="ffn_postnorm",
    )(x, m4, w_in, w_in, w_out, ln_g.reshape(1, d), ln_b.reshape(1, d))


def _mixin_kernel(x_ref, m_ref, w_ref, wrg_ref, wgk_ref, bgk_ref,
                  qm_ref, km_ref, vm_ref, qg_ref, kg_ref, vg_ref, gg_ref, la_ref,
                  *, pos_base, tiles_per_batch, kv_transposed):
    i = pl.program_id(0)
    bb, ts, d = x_ref.shape
    tm = bb * ts
    h = (x_ref[...] * (1.0 + m_ref[1]) + m_ref[0]).reshape(tm, d).astype(BF16)
    p = jnp.dot(h, w_ref[...], preferred_element_type=F32)

    lane = lax.broadcasted_iota(jnp.int32, (1, LANES), 1)
    fi = (lane & (ROPE_HALF - 1)).astype(F32)
    inv = jnp.power(jnp.full((1, LANES), ROPE_THETA, F32), -fi / ROPE_HALF)
    row = lax.broadcasted_iota(jnp.int32, (tm, 1), 0)
    pos = (pos_base + (i % tiles_per_batch) * ts + (row & (ts - 1))).astype(F32)
    ang = pos * inv
    cos = jnp.cos(ang)
    sin = jnp.sin(ang)
    l64 = lane & (MOBA_HEAD_DIM - 1)
    c_tab = jnp.where(l64 < ROPE_DIMS, cos, 1.0)
    s_lo = jnp.where(l64 < ROPE_HALF, -sin, 0.0)
    s_hi = jnp.where((l64 >= ROPE_HALF) & (l64 < ROPE_DIMS), sin, 0.0)

    def rope(x):
        slabs = []
        for s in range(x.shape[1] // LANES):
            xs = x[:, s * LANES:(s + 1) * LANES]
            slabs.append(xs * c_tab + pltpu.roll(xs, LANES - ROPE_HALF, 1) * s_lo
                         + pltpu.roll(xs, ROPE_HALF, 1) * s_hi)
        return jnp.concatenate(slabs, axis=1)

    def store(ref, x):
        ref[...] = x.reshape(ref.shape)

    def store_kv(ref, x):
        if kv_transposed:
            ref[0] = x.T
        else:
            store(ref, x)

    o = 0
    store(qm_ref, rope(p[:, o:o + D_MOBA])); o += D_MOBA
    store_kv(km_ref, rope(p[:, o:o + D_MOBA])); o += D_MOBA
    store_kv(vm_ref, p[:, o:o + D_MOBA]); o += D_MOBA
    store(qg_ref, p[:, o:o + D_GLA_K] * (GLA_DK ** -0.5)); o += D_GLA_K
    store(kg_ref, p[:, o:o + D_GLA_K]); o += D_GLA_K
    store(vg_ref, p[:, o:o + D_GLA]); o += D_GLA
    store(gg_ref, p[:, o:o + D_GLA]); o += D_GLA

    rg = jnp.dot(h, wrg_ref[...], preferred_element_type=F32)
    z = _dot3(rg, wgk_ref[...]) + bgk_ref[...]
    log_sig = jnp.minimum(z, 0.0) - jnp.log1p(jnp.exp(-jnp.abs(z)))
    store(la_ref, log_sig / GLA_GATE_NORM)


def _mixer_in(x, m4, m_row0, w_main, w_rg, w_gk, b_gk, pos_base, kv_transposed):
    nb, s, d = x.shape
    bb, ts = _row_tiling(x)
    tpb = s // ts
    assert m_row0 % bb == 0 and (bb == 1 or not kv_transposed)
    grid = ((nb // bb) * tpb,)
    xmap = lambda i: (i // tpb, i % tpb, 0)
    tmap = lambda i: (i // tpb, 0, i % tpb)
    mmap = lambda i: (1, m_row0 // bb + i // tpb, 0, 0)
    const2 = lambda i: (0, 0)
    widths = (D_MOBA, D_MOBA, D_MOBA, D_GLA_K, D_GLA_K, D_GLA, D_GLA, D_GLA_K)
    out_shape = [jax.ShapeDtypeStruct((nb, s, w), F32) for w in widths]
    out_specs = [pl.BlockSpec((bb, ts, w), xmap) for w in widths]
    if kv_transposed:
        for n in (1, 2):
            out_shape[n] = jax.ShapeDtypeStruct((nb, D_MOBA, s), F32)
            out_specs[n] = pl.BlockSpec((1, D_MOBA, ts), tmap)
    return pl.pallas_call(
        functools.partial(_mixin_kernel, pos_base=pos_base, tiles_per_batch=tpb,
                          kv_transposed=kv_transposed),
        out_shape=out_shape,
        grid=grid,
        in_specs=[pl.BlockSpec((bb, ts, d), xmap),
                  pl.BlockSpec((3, bb, 1, d), mmap),
                  pl.BlockSpec(w_main.shape, const2),
                  pl.BlockSpec(w_rg.shape, const2),
                  pl.BlockSpec(w_gk.shape, const2),
                  pl.BlockSpec((1, D_GLA_K), const2)],
        out_specs=out_specs,
        compiler_params=_cparams(("parallel",)),
        name="mixer_in",
    )(x, m4, w_main, w_rg, w_gk, b_gk.reshape(1, -1))


def _softmax_step(s, pmask, pv, m_scr, l_scr, acc_scr):
    m_old = m_scr[...]
    m_new = jnp.maximum(m_old, jnp.max(jnp.where(pmask, s, NEG_BIG), axis=1, keepdims=True))
    p = jnp.where(pmask, jnp.exp(s - m_new), 0.0)
    alpha = jnp.exp(m_old - m_new)
    l_scr[...] = alpha * l_scr[...] + jnp.sum(p, axis=1, keepdims=True)
    acc_scr[...] = alpha * acc_scr[...] + pv(p)
    m_scr[...] = m_new


def _moba_prompt_kernel(q_ref, kt_ref, vt_ref, o_ref, kmean_scr):
    i = pl.program_id(2)
    blk = q_ref.shape[1]
    s_len = kt_ref.shape[2]
    n_blocks = s_len // blk
    nb8 = kmean_scr.shape[1]
    hd = MOBA_HEAD_DIM
    blk_shift = blk.bit_length() - 1
    blk_row = lax.broadcasted_iota(jnp.int32, (nb8, 1), 0)

    @pl.when(i == 0)
    def _():
        blk_of_key = lax.shift_right_logical(lax.broadcasted_iota(jnp.int32, (1, s_len), 1), blk_shift)
        pool = jnp.where(blk_row == blk_of_key, 1.0 / blk, 0.0).astype(BF16)
        for h in range(2):
            kth = kt_ref[0, h * hd:(h + 1) * hd, :]
            hi = kth.astype(BF16)
            r1 = kth - hi.astype(F32)
            mid = r1.astype(BF16)
            lo = (r1 - mid.astype(F32)).astype(BF16)
            kmean_scr[h] = _dot_nt(pool, hi) + (_dot_nt(pool, mid) + _dot_nt(pool, lo))

    q = q_ref[0]
    row = lax.broadcasted_iota(jnp.int32, (blk, 1), 0)
    col = lax.broadcasted_iota(jnp.int32, (1, blk), 1)
    causal = col <= row
    qs, sels = [], []
    for h in range(2):
        qh = q[:, h * hd:(h + 1) * hd]
        gt = jnp.where(blk_row < i, _dot3(kmean_scr[h], qh, nt=True), -jnp.inf)
        rank = jnp.zeros(gt.shape, F32)
        for m in range(n_blocks):
            gm = gt[m:m + 1, :]
            beats = (gm > gt) | ((gm == gt) & (m < blk_row))
            rank += jnp.where(beats, 1.0, 0.0)
        sel_t = jnp.where((blk_row < i) & (rank < MOBA_TOPK), 1.0, 0.0)
        sels.append(_pad_rows(sel_t, LANES).T)
        qs.append((qh * (hd ** -0.5)).astype(BF16))

    def attend(own):
        n = (own + 1) * blk
        outs = []
        for h in range(2):
            s = _dot(qs[h], kt_ref[0, h * hd:(h + 1) * hd, 0:n])
            parts = []
            for j in range(own):
                picked = jnp.broadcast_to(sels[h][:, j:j + 1], (blk, blk)) > 0.5
                parts.append(jnp.where(picked, s[:, j * blk:(j + 1) * blk], -jnp.inf))
            parts.append(jnp.where(causal, s[:, own * blk:n], -jnp.inf))
            sm = jnp.concatenate(parts, axis=1)
            p = jnp.exp(sm - jnp.max(sm, axis=1, keepdims=True))
            l = jnp.sum(p, axis=1, keepdims=True)
            outs.append(_dot_nt(p, vt_ref[0, h * hd:(h + 1) * hd, 0:n]) / l)
        return jnp.concatenate(outs, axis=1)

    for own in range(n_blocks):
        @pl.when(i == own)
        def _(own=own):
            o_ref[0] = attend(own)


def _moba_prompt(qm, kt, vt):
    nb, s, _ = qm.shape
    blk = MOBA_BLOCK
    assert s % blk == 0 and s // blk <= LANES
    n_pairs = D_MOBA // LANES
    nb8 = -(-(s // blk) // 8) * 8
    qmap = lambda b, hp, i: (b, i, hp)
    kmap = lambda b, hp, i: (b, hp, 0)
    return pl.pallas_call(
        _moba_prompt_kernel,
        out_shape=jax.ShapeDtypeStruct((nb, s, D_MOBA), F32),
        grid=(nb, n_pairs, s // blk),
        in_specs=[pl.BlockSpec((1, blk, LANES), qmap),
                  pl.BlockSpec((1, LANES, s), kmap),
                  pl.BlockSpec((1, LANES, s), kmap)],
        out_specs=pl.BlockSpec((1, blk, LANES), qmap),
        scratch_shapes=[pltpu.VMEM((2, nb8, MOBA_HEAD_DIM), F32)],
        compiler_params=_cparams(("parallel", "parallel", "arbitrary")),
        name="moba_prompt",
    )(qm, kt, vt)


def _gla_out(o, gg, ng):
    ms = jnp.mean(o * o, axis=1, keepdims=True)
    return o * lax.rsqrt(ms + RMS_EPS) * ng * _silu(gg)


def _gla_prompt_kernel(q_ref, k_ref, la_ref, v_ref, gg_ref, ng_ref, o_ref, s_ref, st_scr):
    c = GLA_CHUNK
    n_chunks = q_ref.shape[1] // c
    lane = lax.broadcasted_iota(jnp.int32, (1, LANES), 1)
    row = lax.broadcasted_iota(jnp.int32, (c, 1), 0)
    col = lax.broadcasted_iota(jnp.int32, (1, c), 1)
    tril = col <= row
    tril_bf = jnp.where(tril, 1.0, 0.0).astype(BF16)
    ng = ng_ref[...]
    st_scr[...] = jnp.zeros_like(st_scr)

    def body(ci, carry):
        r0 = pl.multiple_of(ci * c, c)
        q = q_ref[0, pl.ds(r0, c), :]
        k = k_ref[0, pl.ds(r0, c), :]
        b = _dot_exact_lhs(tril_bf, la_ref[0, pl.ds(r0, c), :])
        b_end = b[c - 1:c, :]
        q_dec = q * jnp.exp(b)
        k_inv = k * jnp.exp(-b)
        k_dec = k * jnp.exp(b_end - b)
        e_end = jnp.exp(b_end)
        for h in range(2):
            hm = (lane >= GLA_DK * h) & (lane < GLA_DK * (h + 1))
            qb = jnp.where(hm, q_dec, 0.0)
            vh = v_ref[0, pl.ds(r0, c), h * GLA_DV:(h + 1) * GLA_DV]
            att = jnp.where(tril, _dot_nt(qb, k_inv), 0.0)
            st = st_scr[h]
            o = _dot(att, vh) + _dot_nt(qb, st)
            st_scr[h] = st * e_end + _dot(vh.T, jnp.where(hm, k_dec, 0.0))
            gg = gg_ref[0, pl.ds(r0, c), h * GLA_DV:(h + 1) * GLA_DV]
            o_ref[0, pl.ds(r0, c), h * GLA_DV:(h + 1) * GLA_DV] = _gla_out(o, gg, ng)
        return carry

    lax.fori_loop(0, n_chunks, body, 0)
    for h in range(2):
        s_ref[0, 0, h] = st_scr[h].T[h * GLA_DK:(h + 1) * GLA_DK, :]


def _gla_prompt(qg, kg, la, vg, gg, norm_g):
    nb, s, _ = qg.shape
    assert s % GLA_CHUNK == 0
    n_pairs = GLA_HEADS // 2
    map3 = lambda b, hp: (b, 0, hp)
    return pl.pallas_call(
        _gla_prompt_kernel,
        out_shape=[jax.ShapeDtypeStruct((nb, s, D_GLA), F32),
                   jax.ShapeDtypeStruct((1, nb, GLA_HEADS, GLA_DK, GLA_DV), F32)],
        grid=(nb, n_pairs),
        in_specs=[pl.BlockSpec((1, s, LANES), map3),
                  pl.BlockSpec((1, s, LANES), map3),
                  pl.BlockSpec((1, s, LANES), map3),
                  pl.BlockSpec((1, s, 2 * GLA_DV), map3),
                  pl.BlockSpec((1, s, 2 * GLA_DV), map3),
                  pl.BlockSpec((1, GLA_DV), lambda b, hp: (0, 0))],
        out_specs=[pl.BlockSpec((1, s, 2 * GLA_DV), map3),
                   pl.BlockSpec((1, 1, 2, GLA_DK, GLA_DV), lambda b, hp: (0, b, hp, 0, 0))],
        scratch_shapes=[pltpu.VMEM((2, GLA_DV, LANES), F32)],
        compiler_params=_cparams(("parallel", "parallel")),
        name="gla_prompt",
    )(qg, kg, la, vg, gg, norm_g.reshape(1, GLA_DV))


def _gla_sample_kernel(q_ref, k_ref, la_ref, v_ref, gg_ref, s0_ref, ng_ref, o_ref, s_ref):
    t = q_ref.shape[1]
    lane = lax.broadcasted_iota(jnp.int32, (1, LANES), 1)
    row = lax.broadcasted_iota(jnp.int32, (t, 1), 0)
    ng = ng_ref[...]
    eye = (lax.broadcasted_iota(jnp.int32, (GLA_DK, 1), 0)
           == lax.broadcasted_iota(jnp.int32, (1, GLA_DK), 1))
    zeros_half = jnp.zeros((GLA_DK, GLA_DV), F32)
    for hp in range(GLA_HEADS // 2):
        sl = slice(hp * LANES, (hp + 1) * LANES)
        q = q_ref[0, :, sl]
        k = k_ref[0, :, sl]
        b = la_ref[0, :, sl]
        sh = 1
        while sh < t:
            b = b + jnp.where(row >= sh, pltpu.roll(b, sh, 0), 0.0)
            sh *= 2
        b_end = b[t - 1:t, :]
        e_end = jnp.exp(b_end)
        q_dec = q * jnp.exp(b)
        k_inv = _pad_rows(k * jnp.exp(-b), LANES)
        k_dec = k * jnp.exp(b_end - b)
        for hh in range(2):
            h = 2 * hp + hh
            hm = (lane >= GLA_DK * hh) & (lane < GLA_DK * (hh + 1))
            qb = jnp.where(hm, q_dec, 0.0)
            vh = _pad_rows(v_ref[0, :, h * GLA_DV:(h + 1) * GLA_DV], LANES)
            s0 = s0_ref[0, h]
            s0_pad = jnp.concatenate([s0, zeros_half] if hh == 0 else [zeros_half, s0], axis=0)
            att = jnp.where(lane <= row, _dot_nt(qb, k_inv), 0.0)
            o = _dot(att, vh) + _dot(qb, s0_pad)
            gg = gg_ref[0, :, h * GLA_DV:(h + 1) * GLA_DV]
            o_ref[0, :, h * GLA_DV:(h + 1) * GLA_DV] = _gla_out(o, gg, ng)
            kd = _pad_rows(jnp.where(hm, k_dec, 0.0), LANES)
            upd = _dot(kd.T, vh)[hh * GLA_DK:(hh + 1) * GLA_DK, :]
            e_h = e_end[:, hh * GLA_DK:(hh + 1) * GLA_DK]
            diag = jnp.where(eye, jnp.broadcast_to(e_h, (GLA_DK, GLA_DK)), 0.0)
            s_ref[0, 0, h] = _dot3(diag, s0) + upd


def _gla_sample(qg, kg, la, vg, gg, s0, norm_g):
    nb, t, _ = qg.shape
    map3 = lambda b: (b, 0, 0)
    return pl.pallas_call(
        _gla_sample_kernel,
        out_shape=[jax.ShapeDtypeStruct((nb, t, D_GLA), F32),
                   jax.ShapeDtypeStruct((1, nb, GLA_HEADS, GLA_DK, GLA_DV), F32)],
        grid=(nb,),
        in_specs=[pl.BlockSpec((1, t, D_GLA_K), map3),
                  pl.BlockSpec((1, t, D_GLA_K), map3),
                  pl.BlockSpec((1, t, D_GLA_K), map3),
                  pl.BlockSpec((1, t, D_GLA), map3),
                  pl.BlockSpec((1, t, D_GLA), map3),
                  pl.BlockSpec((1, GLA_HEADS, GLA_DK, GLA_DV), lambda b: (b, 0, 0, 0)),
                  pl.BlockSpec((1, GLA_DV), lambda b: (0, 0))],
        out_specs=[pl.BlockSpec((1, t, D_GLA), map3),
                   pl.BlockSpec((1, 1, GLA_HEADS, GLA_DK, GLA_DV), lambda b: (0, b, 0, 0, 0))],
        compiler_params=_cparams(("parallel",)),
        name="gla_sample",
    )(qg, kg, la, vg, gg, s0, norm_g.reshape(1, GLA_DV))


def _block_diag_queries(q):
    lane = lax.broadcasted_iota(jnp.int32, (1, D_MOBA), 1)
    parts = [jnp.where((lane >= MOBA_HEAD_DIM * h) & (lane < MOBA_HEAD_DIM * (h + 1)), q, 0.0)
             for h in range(MOBA_HEADS)]
    return jnp.concatenate(parts, axis=0)


def _moba_scores_kernel(pt_ref, q_ref, *refs):
    del pt_ref
    n = PAGES_PER_STEP
    pages, s_ref, g_ref = refs[:n], refs[n], refs[n + 1]
    qbd = (_block_diag_queries(q_ref[0]) * (MOBA_HEAD_DIM ** -0.5)).astype(BF16)
    lane = lax.broadcasted_iota(jnp.int32, (1, LANES), 1)
    for p in range(n):
        s_ref[0, :, p * PAGE_SIZE:(p + 1) * PAGE_SIZE] = _dot(qbd, pages[p][0])
    g = jnp.zeros((qbd.shape[0], LANES), F32)
    for j in range(n * PAGE_SIZE // MOBA_BLOCK):
        bs = jnp.sum(s_ref[0, :, j * MOBA_BLOCK:(j + 1) * MOBA_BLOCK], axis=1, keepdims=True)
        g = jnp.where(lane == j, bs, g)
    g_ref[0, 0] = g


def _moba_attend_kernel(pt_ref, q_ref, g_ref, sc_ref, kn_ref, vn_ref, *refs, n_full):
    del pt_ref
    n = PAGES_PER_STEP
    pages, o_ref = refs[:n], refs[n]
    sel_scr, m_scr, l_scr, acc_scr = refs[n + 1:]
    s = pl.program_id(1)
    t = q_ref.shape[1]
    rows = MOBA_HEADS * t
    keys = n * PAGE_SIZE
    bps = keys // MOBA_BLOCK
    lane = lax.broadcasted_iota(jnp.int32, (1, LANES), 1)
    lane_f = lane.astype(F32)

    @pl.when(s == 0)
    def _():
        g = jnp.zeros((rows, LANES), F32)
        for st in range(g_ref.shape[1]):
            g = g + pltpu.roll(g_ref[0, st], st * bps, 1)
        g = jnp.where(lane < n_full, g, -jnp.inf)
        sel = jnp.zeros((rows, LANES), F32)
        for _ in range(MOBA_TOPK):
            mx = jnp.max(g, axis=1, keepdims=True)
            first = jnp.min(jnp.where(g == mx, lane_f, float(LANES)), axis=1, keepdims=True)
            pick = lane_f == first
            sel = jnp.where(pick, 1.0, sel)
            g = jnp.where(pick, -jnp.inf, g)
        sel_scr[...] = sel
        m_scr[...] = jnp.full(m_scr.shape, NEG_BIG, F32)
        l_scr[...] = jnp.zeros_like(l_scr)
        acc_scr[...] = jnp.zeros_like(acc_scr)

    col = lax.broadcasted_iota(jnp.int32, (1, keys), 1)
    blk_of_col = lax.shift_right_logical(col, MOBA_BLOCK.bit_length() - 1) + s * bps
    expand = jnp.where(lax.broadcasted_iota(jnp.int32, (LANES, 1), 0) == blk_of_col, 1.0, 0.0)
    pmask = _dot(sel_scr[...], expand) > 0.5

    def pv(p):
        p = p.astype(BF16)
        out = jnp.zeros(acc_scr.shape, F32)
        for pg in range(n):
            out += _dot_nt(p[:, pg * PAGE_SIZE:(pg + 1) * PAGE_SIZE], pages[pg][0])
        return out

    _softmax_step(sc_ref[0], pmask, pv, m_scr, l_scr, acc_scr)

    @pl.when(s == pl.num_programs(1) - 1)
    def _():
        qbd = _block_diag_queries(q_ref[0]) * (MOBA_HEAD_DIM ** -0.5)
        row_t = lax.broadcasted_iota(jnp.int32, (rows, 1), 0) & (t - 1)
        s_own = _dot_nt(qbd, _pad_rows(kn_ref[0], LANES))
        v_own = _pad_rows(vn_ref[0], LANES)
        _softmax_step(s_own, lane <= row_t, lambda p: _dot(p, v_own), m_scr, l_scr, acc_scr)
        out = acc_scr[...] / l_scr[...]
        lane_w = lax.broadcasted_iota(jnp.int32, (1, D_MOBA), 1)
        o = jnp.zeros((t, D_MOBA), F32)
        for h in range(MOBA_HEADS):
            hm = (lane_w >= MOBA_HEAD_DIM * h) & (lane_w < MOBA_HEAD_DIM * (h + 1))
            o = o + jnp.where(hm, out[h * t:(h + 1) * t, :], 0.0)
        o_ref[0] = o


def _moba_sample(qm, km, vm, cache_k, cache_v, page_table):
    nb, t, _ = qm.shape
    n_pages = page_table.shape[1]
    past = n_pages * PAGE_SIZE
    n_full = past // MOBA_BLOCK
    assert past % MOBA_BLOCK == 0 and MOBA_TOPK <= n_full <= LANES and t & (t - 1) == 0
    n = PAGES_PER_STEP
    assert n_pages % n == 0 and (n * PAGE_SIZE) % MOBA_BLOCK == 0
    steps = n_pages // n
    keys = n * PAGE_SIZE
    rows = MOBA_HEADS * t
    kt = jnp.transpose(cache_k, (0, 2, 3, 1)).reshape(cache_k.shape[0], D_MOBA, PAGE_SIZE)
    vt = jnp.transpose(cache_v, (0, 2, 3, 1)).reshape(cache_v.shape[0], D_MOBA, PAGE_SIZE)

    def page_spec(p):
        return pl.BlockSpec((1, D_MOBA, PAGE_SIZE), lambda b, s, pt, p=p: (pt[b, s * n + p], 0, 0))

    qspec = pl.BlockSpec((1, t, D_MOBA), lambda b, s, pt: (b, 0, 0))
    scores, gates = pl.pallas_call(
        _moba_scores_kernel,
        out_shape=[jax.ShapeDtypeStruct((nb, rows, past), F32),
                   jax.ShapeDtypeStruct((nb, steps, rows, LANES), F32)],
        grid_spec=pltpu.PrefetchScalarGridSpec(
            num_scalar_prefetch=1, grid=(nb, steps),
            in_specs=[qspec] + [page_spec(p) for p in range(n)],
            out_specs=[pl.BlockSpec((1, rows, keys), lambda b, s, pt: (b, 0, s)),
                       pl.BlockSpec((1, 1, rows, LANES), lambda b, s, pt: (b, s, 0, 0))]),
        compiler_params=_cparams(("parallel", "arbitrary")),
        name="moba_sample_scores",
    )(page_table, qm, *([kt] * n))

    return pl.pallas_call(
        functools.partial(_moba_attend_kernel, n_full=n_full),
        out_shape=jax.ShapeDtypeStruct((nb, t, D_MOBA), F32),
        grid_spec=pltpu.PrefetchScalarGridSpec(
            num_scalar_prefetch=1, grid=(nb, steps),
            in_specs=[qspec,
                      pl.BlockSpec((1, steps, rows, LANES), lambda b, s, pt: (b, 0, 0, 0)),
                      pl.BlockSpec((1, rows, keys), lambda b, s, pt: (b, 0, s)),
                      qspec, qspec] + [page_spec(p) for p in range(n)],
            out_specs=qspec,
            scratch_shapes=[pltpu.VMEM((rows, LANES), F32), pltpu.VMEM((rows, 1), F32),
                            pltpu.VMEM((rows, 1), F32), pltpu.VMEM((rows, D_MOBA), F32)]),
        compiler_params=_cparams(("parallel", "arbitrary")),
        name="moba_sample_attend",
    )(page_table, qm, gates, scores, km, vm, *([vt] * n))


def _mixout_kernel(x_ref, om_ref, og_ref, m_ref, w_ref, g_ref, b_ref, o_ref, *, alpha):
    bb, ts, d = x_ref.shape
    tm = bb * ts
    mix = (_dot(om_ref[...].reshape(tm, D_MOBA), w_ref[0:D_MOBA, :])
           + _dot(og_ref[...].reshape(tm, D_GLA), w_ref[D_MOBA:D_MOBA + D_GLA, :]))
    y = alpha * x_ref[...] + (1.0 + m_ref[2]) * mix.reshape(bb, ts, d)
    o_ref[...] = _layer_norm(y, g_ref[...], b_ref[...])


def _mixer_out(x, o_moba, o_gla, m4, m_row0, w_out, ln_g, ln_b, alpha):
    nb, s, d = x.shape
    bb, ts = _row_tiling(x)
    tpb = s // ts
    assert m_row0 % bb == 0
    xmap = lambda i: (i // tpb, i % tpb, 0)
    const2 = lambda i: (0, 0)
    return pl.pallas_call(
        functools.partial(_mixout_kernel, alpha=alpha),
        out_shape=jax.ShapeDtypeStruct(x.shape, F32),
        grid=((nb // bb) * tpb,),
        in_specs=[pl.BlockSpec((bb, ts, d), xmap),
                  pl.BlockSpec((bb, ts, D_MOBA), xmap),
                  pl.BlockSpec((bb, ts, D_GLA), xmap),
                  pl.BlockSpec((3, bb, 1, d), lambda i: (1, m_row0 // bb + i // tpb, 0, 0)),
                  pl.BlockSpec(w_out.shape, const2),
                  pl.BlockSpec((1, d), const2),
                  pl.BlockSpec((1, d), const2)],
        out_specs=pl.BlockSpec((bb, ts, d), xmap),
        compiler_params=_cparams(("parallel",)),
        name="mixer_out_postnorm",
    )(x, o_moba, o_gla, m4, w_out, ln_g.reshape(1, d), ln_b.reshape(1, d))


def kernel(x_prompt, x_sample, cache_k, cache_v, state_gla, page_table, c_prompt, c_sample, w_ada, b_ada, ln_g, ln_b, w_ffn1_in, w_ffn1_out, w_mix_in, w_gk_up, b_gk, gla_norm_g, w_mix_out, w_ffn2_in, w_ffn2_out):
    depth = w_ada.shape[0]
    assert depth == 1, "one decoder layer"
    d = x_prompt.shape[-1]
    alpha = (2.0 * depth) ** 0.25
    nb_p, s_p, _ = x_prompt.shape
    nb_s, t_s, _ = x_sample.shape
    past_len = page_table.shape[1] * PAGE_SIZE

    w1_in, w1_out = w_ffn1_in[0].astype(BF16), w_ffn1_out[0].astype(BF16)
    w2_in, w2_out = w_ffn2_in[0].astype(BF16), w_ffn2_out[0].astype(BF16)
    w_mix = w_mix_in[0]
    w_main = w_mix[:, :D_MIX_MAIN].astype(BF16)
    w_rg = jnp.pad(w_mix[:, D_MIX_MAIN:], ((0, 0), (0, LANES - GLA_GATE_RANK))).astype(BF16)
    w_gk = jnp.pad(w_gk_up[0], ((0, LANES - GLA_GATE_RANK), (0, 0)))
    w_mo = w_mix_out[0].astype(BF16)

    c_all = jnp.concatenate([c_sample, c_prompt], axis=0)
    m = _ada_modulation(c_all, w_ada[0], b_ada[0])
    m4 = m.reshape(3 * N_SUBLAYERS, c_all.shape[0], 1, d)

    def layer(x, m_row0, pos_base, kv_transposed, moba_fn, gla_fn):
        x = _ffn(x, m4, m_row0, 0, w1_in, w1_out, ln_g[0, 0], ln_b[0, 0], alpha)
        qm, km, vm, qg, kg, vg, gg, la = _mixer_in(x, m4, m_row0, w_main, w_rg, w_gk, b_gk[0],
                                                    pos_base, kv_transposed)
        o_moba = moba_fn(qm, km, vm)
        o_gla, s_fin = gla_fn(qg, kg, la, vg, gg)
        x = _mixer_out(x, o_moba, o_gla, m4, m_row0, w_mo, ln_g[0, 1], ln_b[0, 1], alpha)
        x = _ffn(x, m4, m_row0, 2, w2_in, w2_out, ln_g[0, 2], ln_b[0, 2], alpha)
        return x, km, vm, s_fin

    yp, kp_t, vp_t, sp = layer(
        x_prompt, nb_s, 0, True, _moba_prompt,
        lambda qg, kg, la, vg, gg: _gla_prompt(qg, kg, la, vg, gg, gla_norm_g[0]))
    ys, ks, vs, ss = layer(
        x_sample, 0, past_len, False,
        lambda qm, km, vm: _moba_sample(qm, km, vm, cache_k[0], cache_v[0], page_table),
        lambda qg, kg, la, vg, gg: _gla_sample(qg, kg, la, vg, gg, state_gla[0], gla_norm_g[0]))

    def rows_major(a_t):
        a = a_t.reshape(nb_p, MOBA_HEADS, MOBA_HEAD_DIM, s_p)
        return jnp.transpose(a, (0, 3, 1, 2))[None]

    def heads(a):
        return a.reshape(1, nb_s, t_s, MOBA_HEADS, MOBA_HEAD_DIM)

    return (yp, ys, rows_major(kp_t), rows_major(vp_t), sp, heads(ks), heads(vs), ss)
```

```python
import functools

import jax
import jax.numpy as jnp
from jax import lax
from jax.experimental import pallas as pl
from jax.experimental.pallas import tpu as pltpu

F32 = jnp.float32
BF16 = jnp.bfloat16

PAGE_SIZE = 128
MOBA_HEADS = 8
MOBA_HEAD_DIM = 64
D_MOBA = MOBA_HEADS * MOBA_HEAD_DIM
MOBA_BLOCK = 256
MOBA_TOPK = 3
ROPE_THETA = 500000.0
ROPE_DIMS = MOBA_HEAD_DIM // 4
ROPE_HALF = ROPE_DIMS // 2
GLA_HEADS = 4
GLA_DK = 64
GLA_DV = 128
D_GLA_K = GLA_HEADS * GLA_DK
D_GLA = GLA_HEADS * GLA_DV
GLA_GATE_RANK = 16
GLA_GATE_NORM = 16.0
D_MIX_MAIN = 3 * D_MOBA + 2 * D_GLA_K + 2 * D_GLA
N_SUBLAYERS = 3
LN_EPS = 1e-5
RMS_EPS = 1e-6

LANES = 128
NEG_BIG = -1e30
VMEM_LIMIT = 56 * 1024 * 1024

FFN_ROWS = 512
GLA_CHUNK = 128
PAGES_PER_STEP = 32


def _cparams(sem):
    return pltpu.CompilerParams(dimension_semantics=sem, vmem_limit_bytes=VMEM_LIMIT)


def _dot(a, b):
    return jnp.dot(a.astype(BF16), b.astype(BF16), preferred_element_type=F32)


def _dot_nt(a, b):
    return lax.dot_general(a.astype(BF16), b.astype(BF16), (((1,), (1,)), ((), ())),
                           preferred_element_type=F32)


def _split2(x):
    hi = x.astype(BF16)
    lo = (x - hi.astype(F32)).astype(BF16)
    return hi, lo


def _dot3(a, b, nt=False):
    d = _dot_nt if nt else _dot
    ah, al = _split2(a)
    bh, bl = _split2(b)
    return d(ah, bh) + (d(ah, bl) + d(al, bh))


def _dot_exact_lhs(lhs_bf16, x):
    hi = x.astype(BF16)
    r1 = x - hi.astype(F32)
    mid = r1.astype(BF16)
    lo = (r1 - mid.astype(F32)).astype(BF16)
    f = functools.partial(jnp.dot, lhs_bf16, preferred_element_type=F32)
    return f(hi) + (f(mid) + f(lo))


def _silu(x):
    return x * jax.nn.sigmoid(x)


def _layer_norm(y, g, b):
    mu = jnp.mean(y, axis=-1, keepdims=True)
    yc = y - mu
    var = jnp.mean(yc * yc, axis=-1, keepdims=True)
    return yc * lax.rsqrt(var + LN_EPS) * g + b


def _pad_rows(x, rows):
    return jnp.concatenate([x, jnp.zeros((rows - x.shape[0], x.shape[1]), x.dtype)], axis=0)


def _ada_kernel(c_ref, w_ref, b_ref, o_ref):
    o_ref[0] = _dot3(_silu(c_ref[...]), w_ref[...]) + b_ref[...]


def _ada_modulation(c_all, w_ada, b_ada):
    nb, d = c_all.shape
    n_out = w_ada.shape[1] // d
    return pl.pallas_call(
        _ada_kernel,
        out_shape=jax.ShapeDtypeStruct((n_out, nb, d), F32),
        grid=(n_out,),
        in_specs=[pl.BlockSpec((nb, d), lambda n: (0, 0)),
                  pl.BlockSpec((d, d), lambda n: (0, n)),
                  pl.BlockSpec((1, d), lambda n: (0, n))],
        out_specs=pl.BlockSpec((1, nb, d), lambda n: (n, 0, 0)),
        compiler_params=_cparams(("arbitrary",)),
        name="ada_modulation",
    )(c_all, w_ada, b_ada.reshape(1, -1))


def _ffn_kernel(*refs, alpha, fused_mixer_out):
    if fused_mixer_out:
        x_ref, om_ref, og_ref, mm_ref, wmo_ref, gm_ref, bm_ref = refs[:7]
        m_ref, wa_ref, wu_ref, wo_ref, g_ref, b_ref, o_ref, h_scr, acc_scr, res_scr = refs[7:]
    else:
        x_ref, m_ref, wa_ref, wu_ref, wo_ref, g_ref, b_ref, o_ref, h_scr, acc_scr = refs
        res_scr = x_ref
    j = pl.program_id(1)
    bb, ts, d = x_ref.shape
    tm = bb * ts

    @pl.when(j == 0)
    def _():
        x = x_ref[...]
        if fused_mixer_out:
            mix = (_dot(om_ref[...].reshape(tm, D_MOBA), wmo_ref[0:D_MOBA, :])
                   + _dot(og_ref[...].reshape(tm, D_GLA), wmo_ref[D_MOBA:D_MOBA + D_GLA, :]))
            x = _layer_norm(alpha * x + (1.0 + mm_ref[2]) * mix.reshape(bb, ts, d), gm_ref[...], bm_ref[...])
            res_scr[...] = x
        h = x * (1.0 + m_ref[1]) + m_ref[0]
        h_scr[...] = h.reshape(tm, d).astype(BF16)
        acc_scr[...] = jnp.zeros_like(acc_scr)

    h = h_scr[...]
    a = jnp.dot(h, wa_ref[...], preferred_element_type=F32)
    u = jnp.dot(h, wu_ref[...], preferred_element_type=F32)
    t = (_silu(a) * u).astype(BF16)
    acc_scr[...] += jnp.dot(t, wo_ref[...], preferred_element_type=F32)

    @pl.when(j == pl.num_programs(1) - 1)
    def _():
        y = alpha * res_scr[...] + (0.5 * (1.0 + m_ref[2])) * acc_scr[...].reshape(bb, ts, d)
        o_ref[...] = _layer_norm(y, g_ref[...], b_ref[...])


def _row_tiling(x):
    nb, s, _ = x.shape
    if s >= FFN_ROWS:
        assert s % FFN_ROWS == 0
        return 1, FFN_ROWS
    assert s % 8 == 0 and FFN_ROWS % s == 0
    bb = min(nb, FFN_ROWS // s)
    assert nb % bb == 0
    return bb, s


def _ffn(x, m4, m_row0, sub, w_in, w_out, ln_g, ln_b, alpha, mixer_out=None, n_chunks=2):
    nb, s, d = x.shape
    bb, ts = _row_tiling(x)
    tpb = s // ts
    d_ff = w_out.shape[0]
    ck = d_ff // n_chunks
    assert ck * n_chunks == d_ff and ck % LANES == 0 and m_row0 % bb == 0
    grid = ((nb // bb) * tpb, n_chunks)
    xmap = lambda i, j: (i // tpb, i % tpb, 0)
    const2 = lambda i, j: (0, 0)

    def mspec(k):
        return pl.BlockSpec((3, bb, 1, d), lambda i, j: (k, m_row0 // bb + i // tpb, 0, 0))

    vec = pl.BlockSpec((1, d), const2)
    in_specs = [pl.BlockSpec((bb, ts, d), xmap)]
    args = [x]
    scratch = [pltpu.VMEM((bb * ts, d), BF16), pltpu.VMEM((bb * ts, d), F32)]
    if mixer_out is not None:
        o_moba, o_gla, w_mo, g_mo, b_mo = mixer_out
        in_specs += [pl.BlockSpec((bb, ts, D_MOBA), xmap), pl.BlockSpec((bb, ts, D_GLA), xmap),
                     mspec(1), pl.BlockSpec(w_mo.shape, const2), vec, vec]
        args += [o_moba, o_gla, m4, w_mo, g_mo.reshape(1, d), b_mo.reshape(1, d)]
        scratch.append(pltpu.VMEM((bb, ts, d), F32))
    in_specs += [mspec(sub),
                 pl.BlockSpec((d, ck), lambda i, j: (0, j)),
                 pl.BlockSpec((d, ck), lambda i, j: (0, j + n_chunks)),
                 pl.BlockSpec((ck, d), lambda i, j: (j, 0)),
                 vec, vec]
    args += [m4, w_in, w_in, w_out, ln_g.reshape(1, d), ln_b.reshape(1, d)]
    return pl.pallas_call(
        functools.partial(_ffn_kernel, alpha=alpha, fused_mixer_out=mixer_out is not None),
        out_shape=jax.ShapeDtypeStruct(x.shape, F32),
        grid=grid,
        in_specs=in_specs,
        out_specs=pl.BlockSpec((bb, ts, d), xmap),
        scratch_shapes=scratch,
        compiler_params=_cparams(("parallel", "arbitrary")),
        name="ffn_postnorm",
    )(*args)


def _rope_table_kernel(o_ref, *, pos_base):
    tt = o_ref.shape[1]
    lane = lax.broadcasted_iota(jnp.int32, (1, LANES), 1)
    fi = (lane & (ROPE_HALF - 1)).astype(F32)
    inv = jnp.power(jnp.full((1, LANES), ROPE_THETA, F32), -fi / ROPE_HALF)
    row = lax.broadcasted_iota(jnp.int32, (tt, 1), 0)
    pos = (pos_base + pl.program_id(0) * tt + row).astype(F32)
    ang = pos * inv
    cos = jnp.cos(ang)
    sin = jnp.sin(ang)
    l64 = lane & (MOBA_HEAD_DIM - 1)
    o_ref[0] = jnp.where(l64 < ROPE_DIMS, cos, 1.0)
    o_ref[1] = jnp.where(l64 < ROPE_HALF, -sin, 0.0)
    o_ref[2] = jnp.where((l64 >= ROPE_HALF) & (l64 < ROPE_DIMS), sin, 0.0)


def _rope_tables(n_pos, tile, pos_base):
    return pl.pallas_call(
        functools.partial(_rope_table_kernel, pos_base=pos_base),
        out_shape=jax.ShapeDtypeStruct((3, n_pos, LANES), F32),
        grid=(n_pos // tile,),
        in_specs=[],
        out_specs=pl.BlockSpec((3, tile, LANES), lambda i: (0, i, 0)),
        compiler_params=_cparams(("arbitrary",)),
        name="rope_tables",
    )()


def _mixin_kernel(x_ref, m_ref, tab_ref, w_ref, wrg_ref, wgk_ref, bgk_ref,
                  qm_ref, km_ref, vm_ref, qg_ref, kg_ref, vg_ref, gg_ref, la_ref,
                  *, kv_transposed):
    bb, ts, d = x_ref.shape
    tm = bb * ts
    h = (x_ref[...] * (1.0 + m_ref[1]) + m_ref[0]).reshape(tm, d).astype(BF16)
    p = jnp.dot(h, w_ref[...], preferred_element_type=F32)
    c_tab, s_lo, s_hi = tab_ref[0], tab_ref[1], tab_ref[2]

    def rope(x):
        slabs = []
        for s in range(x.shape[1] // LANES):
            xs = x[:, s * LANES:(s + 1) * LANES]
            r = (xs.reshape(bb, ts, LANES) * c_tab
                 + pltpu.roll(xs, LANES - ROPE_HALF, 1).reshape(bb, ts, LANES) * s_lo
                 + pltpu.roll(xs, ROPE_HALF, 1).reshape(bb, ts, LANES) * s_hi)
            slabs.append(r.reshape(tm, LANES))
        return jnp.concatenate(slabs, axis=1)

    def store(ref, x):
        ref[...] = x.reshape(ref.shape)

    def store_kv(ref, x):
        if kv_transposed:
            ref[0] = x.T
        else:
            store(ref, x)

    o = 0
    store(qm_ref, rope(p[:, o:o + D_MOBA])); o += D_MOBA
    store_kv(km_ref, rope(p[:, o:o + D_MOBA])); o += D_MOBA
    store_kv(vm_ref, p[:, o:o + D_MOBA]); o += D_MOBA
    store(qg_ref, p[:, o:o + D_GLA_K] * (GLA_DK ** -0.5)); o += D_GLA_K
    store(kg_ref, p[:, o:o + D_GLA_K]); o += D_GLA_K
    store(vg_ref, p[:, o:o + D_GLA]); o += D_GLA
    store(gg_ref, p[:, o:o + D_GLA]); o += D_GLA

    rg = jnp.dot(h, wrg_ref[...], preferred_element_type=F32)
    z = _dot3(rg, wgk_ref[...]) + bgk_ref[...]
    log_sig = jnp.minimum(z, 0.0) - jnp.log1p(jnp.exp(-jnp.abs(z)))
    store(la_ref, log_sig / GLA_GATE_NORM)


def _mixer_in(x, m4, m_row0, w_main, w_rg, w_gk, b_gk, pos_base, kv_transposed):
    nb, s, d = x.shape
    bb, ts = _row_tiling(x)
    tpb = s // ts
    assert m_row0 % bb == 0 and (bb == 1 or not kv_transposed)
    grid = ((nb // bb) * tpb,)
    xmap = lambda i: (i // tpb, i % tpb, 0)
    tmap = lambda i: (i // tpb, 0, i % tpb)
    mmap = lambda i: (1, m_row0 // bb + i // tpb, 0, 0)
    const2 = lambda i: (0, 0)
    widths = (D_MOBA, D_MOBA, D_MOBA, D_GLA_K, D_GLA_K, D_GLA, D_GLA, D_GLA_K)
    out_shape = [jax.ShapeDtypeStruct((nb, s, w), F32) for w in widths]
    out_specs = [pl.BlockSpec((bb, ts, w), xmap) for w in widths]
    if kv_transposed:
        for n in (1, 2):
            out_shape[n] = jax.ShapeDtypeStruct((nb, D_MOBA, s), F32)
            out_specs[n] = pl.BlockSpec((1, D_MOBA, ts), tmap)
    tables = _rope_tables(s, ts, pos_base)
    return pl.pallas_call(
        functools.partial(_mixin_kernel, kv_transposed=kv_transposed),
        out_shape=out_shape,
        grid=grid,
        in_specs=[pl.BlockSpec((bb, ts, d), xmap),
                  pl.BlockSpec((3, bb, 1, d), mmap),
                  pl.BlockSpec((3, ts, LANES), lambda i: (0, i % tpb, 0)),
                  pl.BlockSpec(w_main.shape, const2),
                  pl.BlockSpec(w_rg.shape, const2),
                  pl.BlockSpec(w_gk.shape, const2),
                  pl.BlockSpec((1, D_GLA_K), const2)],
        out_specs=out_specs,
        compiler_params=_cparams(("parallel",)),
        name="mixer_in",
    )(x, m4, tables, w_main, w_rg, w_gk, b_gk.reshape(1, -1))


def _softmax_step(s, pmask, pv, m_scr, l_scr, acc_scr):
    m_old = m_scr[...]
    m_new = jnp.maximum(m_old, jnp.max(jnp.where(pmask, s, NEG_BIG), axis=1, keepdims=True))
    p = jnp.where(pmask, jnp.exp(s - m_new), 0.0)
    alpha = jnp.exp(m_old - m_new)
    l_scr[...] = alpha * l_scr[...] + jnp.sum(p, axis=1, keepdims=True)
    acc_scr[...] = alpha * acc_scr[...] + pv(p)
    m_scr[...] = m_new


def _moba_prompt_kernel(q_ref, kt_ref, vt_ref, o_ref, kmean_scr):
    i = pl.program_id(2)
    blk = q_ref.shape[1]
    s_len = kt_ref.shape[2]
    n_blocks = s_len // blk
    nb8 = kmean_scr.shape[1]
    hd = MOBA_HEAD_DIM
    blk_shift = blk.bit_length() - 1
    blk_row = lax.broadcasted_iota(jnp.int32, (nb8, 1), 0)

    @pl.when(i == 0)
    def _():
        blk_of_key = lax.shift_right_logical(lax.broadcasted_iota(jnp.int32, (1, s_len), 1), blk_shift)
        pool = jnp.where(blk_row == blk_of_key, 1.0 / blk, 0.0).astype(BF16)
        for h in range(2):
            kth = kt_ref[0, h * hd:(h + 1) * hd, :]
            hi = kth.astype(BF16)
            r1 = kth - hi.astype(F32)
            mid = r1.astype(BF16)
            lo = (r1 - mid.astype(F32)).astype(BF16)
            kmean_scr[h] = _dot_nt(pool, hi) + (_dot_nt(pool, mid) + _dot_nt(pool, lo))

    q = q_ref[0]
    row = lax.broadcasted_iota(jnp.int32, (blk, 1), 0)
    col = lax.broadcasted_iota(jnp.int32, (1, blk), 1)
    causal = col <= row
    qs, sels = [], []
    for h in range(2):
        qh = q[:, h * hd:(h + 1) * hd]
        gt = jnp.where(blk_row < i, _dot3(kmean_scr[h], qh, nt=True), -jnp.inf)
        rank = jnp.zeros(gt.shape, F32)
        for m in range(n_blocks):
            gm = gt[m:m + 1, :]
            beats = (gm > gt) | ((gm == gt) & (m < blk_row))
            rank += jnp.where(beats, 1.0, 0.0)
        sel_t = jnp.where((blk_row < i) & (rank < MOBA_TOPK), 1.0, 0.0)
        sels.append(_pad_rows(sel_t, LANES).T)
        qs.append((qh * (hd ** -0.5)).astype(BF16))

    def attend(own):
        n = (own + 1) * blk
        outs = []
        for h in range(2):
            s = _dot(qs[h], kt_ref[0, h * hd:(h + 1) * hd, 0:n])
            parts = []
            for j in range(own):
                picked = jnp.broadcast_to(sels[h][:, j:j + 1], (blk, blk)) > 0.5
                parts.append(jnp.where(picked, s[:, j * blk:(j + 1) * blk], -jnp.inf))
            parts.append(jnp.where(causal, s[:, own * blk:n], -jnp.inf))
            sm = jnp.concatenate(parts, axis=1)
            p = jnp.exp(sm - jnp.max(sm, axis=1, keepdims=True))
            l = jnp.sum(p, axis=1, keepdims=True)
            outs.append(_dot_nt(p, vt_ref[0, h * hd:(h + 1) * hd, 0:n]) / l)
        return jnp.concatenate(outs, axis=1)

    for own in range(n_blocks):
        @pl.when(i == own)
        def _(own=own):
            o_ref[0] = attend(own)


def _moba_prompt(qm, kt, vt):
    nb, s, _ = qm.shape
    blk = MOBA_BLOCK
    assert s % blk == 0 and s // blk <= LANES
    n_pairs = D_MOBA // LANES
    nb8 = -(-(s // blk) // 8) * 8
    qmap = lambda b, hp, i: (b, i, hp)
    kmap = lambda b, hp, i: (b, hp, 0)
    return pl.pallas_call(
        _moba_prompt_kernel,
        out_shape=jax.ShapeDtypeStruct((nb, s, D_MOBA), F32),
        grid=(nb, n_pairs, s // blk),
        in_specs=[pl.BlockSpec((1, blk, LANES), qmap),
                  pl.BlockSpec((1, LANES, s), kmap),
                  pl.BlockSpec((1, LANES, s), kmap)],
        out_specs=pl.BlockSpec((1, blk, LANES), qmap),
        scratch_shapes=[pltpu.VMEM((2, nb8, MOBA_HEAD_DIM), F32)],
        compiler_params=_cparams(("parallel", "parallel", "arbitrary")),
        name="moba_prompt",
    )(qm, kt, vt)


def _gla_out(o, gg, ng):
    ms = jnp.mean(o * o, axis=1, keepdims=True)
    return o * lax.rsqrt(ms + RMS_EPS) * ng * _silu(gg)


def _gla_prompt_kernel(q_ref, k_ref, la_ref, v_ref, gg_ref, ng_ref, o_ref, s_ref, st_scr):
    c = GLA_CHUNK
    n_chunks = q_ref.shape[1] // c
    lane = lax.broadcasted_iota(jnp.int32, (1, LANES), 1)
    row = lax.broadcasted_iota(jnp.int32, (c, 1), 0)
    col = lax.broadcasted_iota(jnp.int32, (1, c), 1)
    tril = col <= row
    tril_bf = jnp.where(tril, 1.0, 0.0).astype(BF16)
    ng = ng_ref[...]
    st_scr[...] = jnp.zeros_like(st_scr)

    def body(ci, carry):
        r0 = pl.multiple_of(ci * c, c)
        q = q_ref[0, pl.ds(r0, c), :]
        k = k_ref[0, pl.ds(r0, c), :]
        b = _dot_exact_lhs(tril_bf, la_ref[0, pl.ds(r0, c), :])
        b_end = b[c - 1:c, :]
        q_dec = q * jnp.exp(b)
        k_inv = k * jnp.exp(-b)
        k_dec = k * jnp.exp(b_end - b)
        e_end = jnp.exp(b_end)
        for h in range(2):
            hm = (lane >= GLA_DK * h) & (lane < GLA_DK * (h + 1))
            qb = jnp.where(hm, q_dec, 0.0)
            vh = v_ref[0, pl.ds(r0, c), h * GLA_DV:(h + 1) * GLA_DV]
            att = jnp.where(tril, _dot_nt(qb, k_inv), 0.0)
            st = st_scr[h]
            o = _dot(att, vh) + _dot_nt(qb, st)
            st_scr[h] = st * e_end + _dot(vh.T, jnp.where(hm, k_dec, 0.0))
            gg = gg_ref[0, pl.ds(r0, c), h * GLA_DV:(h + 1) * GLA_DV]
            o_ref[0, pl.ds(r0, c), h * GLA_DV:(h + 1) * GLA_DV] = _gla_out(o, gg, ng)
        return carry

    lax.fori_loop(0, n_chunks, body, 0, unroll=2)
    for h in range(2):
        s_ref[0, 0, h] = st_scr[h].T[h * GLA_DK:(h + 1) * GLA_DK, :]


def _gla_prompt(qg, kg, la, vg, gg, norm_g):
    nb, s, _ = qg.shape
    assert s % GLA_CHUNK == 0
    n_pairs = GLA_HEADS // 2
    map3 = lambda b, hp: (b, 0, hp)
    return pl.pallas_call(
        _gla_prompt_kernel,
        out_shape=[jax.ShapeDtypeStruct((nb, s, D_GLA), F32),
                   jax.ShapeDtypeStruct((1, nb, GLA_HEADS, GLA_DK, GLA_DV), F32)],
        grid=(nb, n_pairs),
        in_specs=[pl.BlockSpec((1, s, LANES), map3),
                  pl.BlockSpec((1, s, LANES), map3),
                  pl.BlockSpec((1, s, LANES), map3),
                  pl.BlockSpec((1, s, 2 * GLA_DV), map3),
                  pl.BlockSpec((1, s, 2 * GLA_DV), map3),
                  pl.BlockSpec((1, GLA_DV), lambda b, hp: (0, 0))],
        out_specs=[pl.BlockSpec((1, s, 2 * GLA_DV), map3),
                   pl.BlockSpec((1, 1, 2, GLA_DK, GLA_DV), lambda b, hp: (0, b, hp, 0, 0))],
        scratch_shapes=[pltpu.VMEM((2, GLA_DV, LANES), F32)],
        compiler_params=_cparams(("parallel", "parallel")),
        name="gla_prompt",
    )(qg, kg, la, vg, gg, norm_g.reshape(1, GLA_DV))


def _gla_sample_kernel(q_ref, k_ref, la_ref, v_ref, gg_ref, s0_ref, ng_ref, o_ref, s_ref):
    t = q_ref.shape[1]
    lane = lax.broadcasted_iota(jnp.int32, (1, LANES), 1)
    row = lax.broadcasted_iota(jnp.int32, (t, 1), 0)
    ng = ng_ref[...]
    eye = (lax.broadcasted_iota(jnp.int32, (GLA_DK, 1), 0)
           == lax.broadcasted_iota(jnp.int32, (1, GLA_DK), 1))
    zeros_half = jnp.zeros((GLA_DK, GLA_DV), F32)
    for hp in range(GLA_HEADS // 2):
        sl = slice(hp * LANES, (hp + 1) * LANES)
        q = q_ref[0, :, sl]
        k = k_ref[0, :, sl]
        b = la_ref[0, :, sl]
        sh = 1
        while sh < t:
            b = b + jnp.where(row >= sh, pltpu.roll(b, sh, 0), 0.0)
            sh *= 2
        b_end = b[t - 1:t, :]
        e_end = jnp.exp(b_end)
        q_dec = q * jnp.exp(b)
        k_inv = _pad_rows(k * jnp.exp(-b), LANES)
        k_dec = k * jnp.exp(b_end - b)
        for hh in range(2):
            h = 2 * hp + hh
            hm = (lane >= GLA_DK * hh) & (lane < GLA_DK * (hh + 1))
            qb = jnp.where(hm, q_dec, 0.0)
            vh = _pad_rows(v_ref[0, :, h * GLA_DV:(h + 1) * GLA_DV], LANES)
            s0 = s0_ref[0, h]
            s0_pad = jnp.concatenate([s0, zeros_half] if hh == 0 else [zeros_half, s0], axis=0)
            att = jnp.where(lane <= row, _dot_nt(qb, k_inv), 0.0)
            o = _dot(att, vh) + _dot(qb, s0_pad)
            gg = gg_ref[0, :, h * GLA_DV:(h + 1) * GLA_DV]
            o_ref[0, :, h * GLA_DV:(h + 1) * GLA_DV] = _gla_out(o, gg, ng)
            kd = _pad_rows(jnp.where(hm, k_dec, 0.0), LANES)
            upd = _dot(kd.T, vh)[hh * GLA_DK:(hh + 1) * GLA_DK, :]
            e_h = e_end[:, hh * GLA_DK:(hh + 1) * GLA_DK]
            diag = jnp.where(eye, jnp.broadcast_to(e_h, (GLA_DK, GLA_DK)), 0.0)
            s_ref[0, 0, h] = _dot3(diag, s0) + upd


def _gla_sample(qg, kg, la, vg, gg, s0, norm_g):
    nb, t, _ = qg.shape
    map3 = lambda b: (b, 0, 0)
    return pl.pallas_call(
        _gla_sample_kernel,
        out_shape=[jax.ShapeDtypeStruct((nb, t, D_GLA), F32),
                   jax.ShapeDtypeStruct((1, nb, GLA_HEADS, GLA_DK, GLA_DV), F32)],
        grid=(nb,),
        in_specs=[pl.BlockSpec((1, t, D_GLA_K), map3),
                  pl.BlockSpec((1, t, D_GLA_K), map3),
                  pl.BlockSpec((1, t, D_GLA_K), map3),
                  pl.BlockSpec((1, t, D_GLA), map3),
                  pl.BlockSpec((1, t, D_GLA), map3),
                  pl.BlockSpec((1, GLA_HEADS, GLA_DK, GLA_DV), lambda b: (b, 0, 0, 0)),
                  pl.BlockSpec((1, GLA_DV), lambda b: (0, 0))],
        out_specs=[pl.BlockSpec((1, t, D_GLA), map3),
                   pl.BlockSpec((1, 1, GLA_HEADS, GLA_DK, GLA_DV), lambda b: (0, b, 0, 0, 0))],
        compiler_params=_cparams(("parallel",)),
        name="gla_sample",
    )(qg, kg, la, vg, gg, s0, norm_g.reshape(1, GLA_DV))


def _block_diag_queries(q):
    lane = lax.broadcasted_iota(jnp.int32, (1, D_MOBA), 1)
    parts = [jnp.where((lane >= MOBA_HEAD_DIM * h) & (lane < MOBA_HEAD_DIM * (h + 1)), q, 0.0)
             for h in range(MOBA_HEADS)]
    return jnp.concatenate(parts, axis=0)


def _moba_scores_kernel(pt_ref, q_ref, *refs):
    del pt_ref
    n = PAGES_PER_STEP
    pages, s_ref, g_ref = refs[:n], refs[n], refs[n + 1]
    qbd = (_block_diag_queries(q_ref[0]) * (MOBA_HEAD_DIM ** -0.5)).astype(BF16)
    lane = lax.broadcasted_iota(jnp.int32, (1, LANES), 1)
    for p in range(n):
        s_ref[0, :, p * PAGE_SIZE:(p + 1) * PAGE_SIZE] = _dot(qbd, pages[p][0])
    g = jnp.zeros((qbd.shape[0], LANES), F32)
    for j in range(n * PAGE_SIZE // MOBA_BLOCK):
        bs = jnp.sum(s_ref[0, :, j * MOBA_BLOCK:(j + 1) * MOBA_BLOCK], axis=1, keepdims=True)
        g = jnp.where(lane == j, bs, g)
    g_ref[0, 0] = g


def _moba_attend_kernel(pt_ref, q_ref, g_ref, sc_ref, kn_ref, vn_ref, *refs, n_full):
    del pt_ref
    n = PAGES_PER_STEP
    pages, o_ref = refs[:n], refs[n]
    sel_scr, m_scr, l_scr, acc_scr = refs[n + 1:]
    s = pl.program_id(1)
    t = q_ref.shape[1]
    rows = MOBA_HEADS * t
    keys = n * PAGE_SIZE
    bps = keys // MOBA_BLOCK
    lane = lax.broadcasted_iota(jnp.int32, (1, LANES), 1)
    lane_f = lane.astype(F32)

    @pl.when(s == 0)
    def _():
        g = jnp.zeros((rows, LANES), F32)
        for st in range(g_ref.shape[1]):
            g = g + pltpu.roll(g_ref[0, st], st * bps, 1)
        g = jnp.where(lane < n_full, g, -jnp.inf)
        sel = jnp.zeros((rows, LANES), F32)
        for _ in range(MOBA_TOPK):
            mx = jnp.max(g, axis=1, keepdims=True)
            first = jnp.min(jnp.where(g == mx, lane_f, float(LANES)), axis=1, keepdims=True)
            pick = lane_f == first
            sel = jnp.where(pick, 1.0, sel)
            g = jnp.where(pick, -jnp.inf, g)
        sel_scr[...] = sel
        m_scr[...] = jnp.full(m_scr.shape, NEG_BIG, F32)
        l_scr[...] = jnp.zeros_like(l_scr)
        acc_scr[...] = jnp.zeros_like(acc_scr)

    col = lax.broadcasted_iota(jnp.int32, (1, keys), 1)
    blk_of_col = lax.shift_right_logical(col, MOBA_BLOCK.bit_length() - 1) + s * bps
    expand = jnp.where(lax.broadcasted_iota(jnp.int32, (LANES, 1), 0) == blk_of_col, 1.0, 0.0)
    pmask = _dot(sel_scr[...], expand) > 0.5

    def pv(p):
        p = p.astype(BF16)
        out = jnp.zeros(acc_scr.shape, F32)
        for pg in range(n):
            out += _dot_nt(p[:, pg * PAGE_SIZE:(pg + 1) * PAGE_SIZE], pages[pg][0])
        return out

    _softmax_step(sc_ref[0], pmask, pv, m_scr, l_scr, acc_scr)

    @pl.when(s == pl.num_programs(1) - 1)
    def _():
        qbd = _block_diag_queries(q_ref[0]) * (MOBA_HEAD_DIM ** -0.5)
        row_t = lax.broadcasted_iota(jnp.int32, (rows, 1), 0) & (t - 1)
        s_own = _dot_nt(qbd, _pad_rows(kn_ref[0], LANES))
        v_own = _pad_rows(vn_ref[0], LANES)
        _softmax_step(s_own, lane <= row_t, lambda p: _dot(p, v_own), m_scr, l_scr, acc_scr)
        out = acc_scr[...] / l_scr[...]
        lane_w = lax.broadcasted_iota(jnp.int32, (1, D_MOBA), 1)
        o = jnp.zeros((t, D_MOBA), F32)
        for h in range(MOBA_HEADS):
            hm = (lane_w >= MOBA_HEAD_DIM * h) & (lane_w < MOBA_HEAD_DIM * (h + 1))
            o = o + jnp.where(hm, out[h * t:(h + 1) * t, :], 0.0)
        o_ref[0] = o


def _moba_sample(qm, km, vm, cache_k, cache_v, page_table):
    nb, t, _ = qm.shape
    n_pages = page_table.shape[1]
    past = n_pages * PAGE_SIZE
    n_full = past // MOBA_BLOCK
    assert past % MOBA_BLOCK == 0 and MOBA_TOPK <= n_full <= LANES and t & (t - 1) == 0
    n = PAGES_PER_STEP
    assert n_pages % n == 0 and (n * PAGE_SIZE) % MOBA_BLOCK == 0
    steps = n_pages // n
    keys = n * PAGE_SIZE
    rows = MOBA_HEADS * t
    kt = jnp.transpose(cache_k, (0, 2, 3, 1)).reshape(cache_k.shape[0], D_MOBA, PAGE_SIZE)
    vt = jnp.transpose(cache_v, (0, 2, 3, 1)).reshape(cache_v.shape[0], D_MOBA, PAGE_SIZE)

    def page_spec(p):
        return pl.BlockSpec((1, D_MOBA, PAGE_SIZE), lambda b, s, pt, p=p: (pt[b, s * n + p], 0, 0))

    qspec = pl.BlockSpec((1, t, D_MOBA), lambda b, s, pt: (b, 0, 0))
    scores, gates = pl.pallas_call(
        _moba_scores_kernel,
        out_shape=[jax.ShapeDtypeStruct((nb, rows, past), F32),
                   jax.ShapeDtypeStruct((nb, steps, rows, LANES), F32)],
        grid_spec=pltpu.PrefetchScalarGridSpec(
            num_scalar_prefetch=1, grid=(nb, steps),
            in_specs=[qspec] + [page_spec(p) for p in range(n)],
            out_specs=[pl.BlockSpec((1, rows, keys), lambda b, s, pt: (b, 0, s)),
                       pl.BlockSpec((1, 1, rows, LANES), lambda b, s, pt: (b, s, 0, 0))]),
        compiler_params=_cparams(("parallel", "arbitrary")),
        name="moba_sample_scores",
    )(page_table, qm, *([kt] * n))

    return pl.pallas_call(
        functools.partial(_moba_attend_kernel, n_full=n_full),
        out_shape=jax.ShapeDtypeStruct((nb, t, D_MOBA), F32),
        grid_spec=pltpu.PrefetchScalarGridSpec(
            num_scalar_prefetch=1, grid=(nb, steps),
            in_specs=[qspec,
                      pl.BlockSpec((1, steps, rows, LANES), lambda b, s, pt: (b, 0, 0, 0)),
                      pl.BlockSpec((1, rows, keys), lambda b, s, pt: (b, 0, s)),
                      qspec, qspec] + [page_spec(p) for p in range(n)],
            out_specs=qspec,
            scratch_shapes=[pltpu.VMEM((rows, LANES), F32), pltpu.VMEM((rows, 1), F32),
                            pltpu.VMEM((rows, 1), F32), pltpu.VMEM((rows, D_MOBA), F32)]),
        compiler_params=_cparams(("parallel", "arbitrary")),
        name="moba_sample_attend",
    )(page_table, qm, gates, scores, km, vm, *([vt] * n))


def kernel(x_prompt, x_sample, cache_k, cache_v, state_gla, page_table, c_prompt, c_sample, w_ada, b_ada, ln_g, ln_b, w_ffn1_in, w_ffn1_out, w_mix_in, w_gk_up, b_gk, gla_norm_g, w_mix_out, w_ffn2_in, w_ffn2_out):
    depth = w_ada.shape[0]
    assert depth == 1, "one decoder layer"
    d = x_prompt.shape[-1]
    alpha = (2.0 * depth) ** 0.25
    nb_p, s_p, _ = x_prompt.shape
    nb_s, t_s, _ = x_sample.shape
    past_len = page_table.shape[1] * PAGE_SIZE

    w1_in, w1_out = w_ffn1_in[0].astype(BF16), w_ffn1_out[0].astype(BF16)
    w2_in, w2_out = w_ffn2_in[0].astype(BF16), w_ffn2_out[0].astype(BF16)
    w_mix = w_mix_in[0]
    w_main = w_mix[:, :D_MIX_MAIN].astype(BF16)
    w_rg = jnp.pad(w_mix[:, D_MIX_MAIN:], ((0, 0), (0, LANES - GLA_GATE_RANK))).astype(BF16)
    w_gk = jnp.pad(w_gk_up[0], ((0, LANES - GLA_GATE_RANK), (0, 0)))
    w_mo = w_mix_out[0].astype(BF16)

    c_all = jnp.concatenate([c_sample, c_prompt], axis=0)
    m = _ada_modulation(c_all, w_ada[0], b_ada[0])
    m4 = m.reshape(3 * N_SUBLAYERS, c_all.shape[0], 1, d)

    def layer(x, m_row0, pos_base, kv_transposed, moba_fn, gla_fn):
        x = _ffn(x, m4, m_row0, 0, w1_in, w1_out, ln_g[0, 0], ln_b[0, 0], alpha)
        qm, km, vm, qg, kg, vg, gg, la = _mixer_in(x, m4, m_row0, w_main, w_rg, w_gk, b_gk[0],
                                                    pos_base, kv_transposed)
        o_moba = moba_fn(qm, km, vm)
        o_gla, s_fin = gla_fn(qg, kg, la, vg, gg)
        x = _ffn(x, m4, m_row0, 2, w2_in, w2_out, ln_g[0, 2], ln_b[0, 2], alpha,
                 mixer_out=(o_moba, o_gla, w_mo, ln_g[0, 1], ln_b[0, 1]))
        return x, km, vm, s_fin

    yp, kp_t, vp_t, sp = layer(
        x_prompt, nb_s, 0, True, _moba_prompt,
        lambda qg, kg, la, vg, gg: _gla_prompt(qg, kg, la, vg, gg, gla_norm_g[0]))
    ys, ks, vs, ss = layer(
        x_sample, 0, past_len, False,
        lambda qm, km, vm: _moba_sample(qm, km, vm, cache_k[0], cache_v[0], page_table),
        lambda qg, kg, la, vg, gg: _gla_sample(qg, kg, la, vg, gg, state_gla[0], gla_norm_g[0]))

    def rows_major(a_t):
        a = a_t.reshape(nb_p, MOBA_HEADS, MOBA_HEAD_DIM, s_p)
        return jnp.transpose(a, (0, 3, 1, 2))[None]

    def heads(a):
        return a.reshape(1, nb_s, t_s, MOBA_HEADS, MOBA_HEAD_DIM)

    return (yp, ys, rows_major(kp_t), rows_major(vp_t), sp, heads(ks), heads(vs), ss)
```

```python
import functools

import jax
import jax.numpy as jnp
from jax import lax
from jax.experimental import pallas as pl
from jax.experimental.pallas import tpu as pltpu

F32 = jnp.float32
BF16 = jnp.bfloat16

PAGE_SIZE = 128
MOBA_HEADS = 8
MOBA_HEAD_DIM = 64
D_MOBA = MOBA_HEADS * MOBA_HEAD_DIM
MOBA_BLOCK = 256
MOBA_TOPK = 3
ROPE_THETA = 500000.0
ROPE_DIMS = MOBA_HEAD_DIM // 4
ROPE_HALF = ROPE_DIMS // 2
GLA_HEADS = 4
GLA_DK = 64
GLA_DV = 128
D_GLA_K = GLA_HEADS * GLA_DK
D_GLA = GLA_HEADS * GLA_DV
GLA_GATE_RANK = 16
GLA_GATE_NORM = 16.0
D_MIX_MAIN = 3 * D_MOBA + 2 * D_GLA_K + 2 * D_GLA
N_SUBLAYERS = 3
LN_EPS = 1e-5
RMS_EPS = 1e-6

LANES = 128
NEG_BIG = -1e30
VMEM_LIMIT = 56 * 1024 * 1024

FFN_ROWS = 512
GLA_CHUNK = 128
PAGES_PER_STEP = 64


def _cparams(sem):
    return pltpu.CompilerParams(dimension_semantics=sem, vmem_limit_bytes=VMEM_LIMIT)


def _dot(a, b):
    return jnp.dot(a.astype(BF16), b.astype(BF16), preferred_element_type=F32)


def _dot_nt(a, b):
    return lax.dot_general(a.astype(BF16), b.astype(BF16), (((1,), (1,)), ((), ())),
                           preferred_element_type=F32)


def _split2(x):
    hi = x.astype(BF16)
    lo = (x - hi.astype(F32)).astype(BF16)
    return hi, lo


def _dot3(a, b, nt=False):
    d = _dot_nt if nt else _dot
    ah, al = _split2(a)
    bh, bl = _split2(b)
    return d(ah, bh) + (d(ah, bl) + d(al, bh))


def _dot_exact_lhs(lhs_bf16, x):
    hi = x.astype(BF16)
    r1 = x - hi.astype(F32)
    mid = r1.astype(BF16)
    lo = (r1 - mid.astype(F32)).astype(BF16)
    f = functools.partial(jnp.dot, lhs_bf16, preferred_element_type=F32)
    return f(hi) + (f(mid) + f(lo))


def _silu(x):
    return x * jax.nn.sigmoid(x)


def _layer_norm(y, g, b):
    mu = jnp.mean(y, axis=-1, keepdims=True)
    yc = y - mu
    var = jnp.mean(yc * yc, axis=-1, keepdims=True)
    return yc * lax.rsqrt(var + LN_EPS) * g + b


def _pad_rows(x, rows):
    return jnp.concatenate([x, jnp.zeros((rows - x.shape[0], x.shape[1]), x.dtype)], axis=0)


def _ada_kernel(c_ref, w_ref, b_ref, o_ref):
    o_ref[0] = _dot3(_silu(c_ref[...]), w_ref[...]) + b_ref[...]


def _ada_modulation(c_all, w_ada, b_ada):
    nb, d = c_all.shape
    n_out = w_ada.shape[1] // d
    return pl.pallas_call(
        _ada_kernel,
        out_shape=jax.ShapeDtypeStruct((n_out, nb, d), F32),
        grid=(n_out,),
        in_specs=[pl.BlockSpec((nb, d), lambda n: (0, 0)),
                  pl.BlockSpec((d, d), lambda n: (0, n)),
                  pl.BlockSpec((1, d), lambda n: (0, n))],
        out_specs=pl.BlockSpec((1, nb, d), lambda n: (n, 0, 0)),
        compiler_params=_cparams(("arbitrary",)),
        name="ada_modulation",
    )(c_all, w_ada, b_ada.reshape(1, -1))


def _ffn_kernel(*refs, alpha, fused_mixer_out, n_chunks):
    if fused_mixer_out:
        x_ref, om_ref, og_ref, mm_ref, wmo_ref, gm_ref, bm_ref = refs[:7]
        m_ref, wa_ref, wu_ref, wo_ref, g_ref, b_ref, o_ref, h_scr, acc_scr, res_scr = refs[7:]
    else:
        x_ref, m_ref, wa_ref, wu_ref, wo_ref, g_ref, b_ref, o_ref, h_scr, acc_scr = refs
        res_scr = x_ref
    j = pl.program_id(1)
    bb, ts, d = x_ref.shape
    tm = bb * ts

    def modulated_input():
        x = x_ref[...]
        if fused_mixer_out:
            mix = (_dot(om_ref[...].reshape(tm, D_MOBA), wmo_ref[0:D_MOBA, :])
                   + _dot(og_ref[...].reshape(tm, D_GLA), wmo_ref[D_MOBA:D_MOBA + D_GLA, :]))
            x = _layer_norm(alpha * x + (1.0 + mm_ref[2]) * mix.reshape(bb, ts, d), gm_ref[...], bm_ref[...])
            res_scr[...] = x
        h = (x * (1.0 + m_ref[1]) + m_ref[0]).reshape(tm, d).astype(BF16)
        h_scr[...] = h
        return h

    def chunk(h):
        a = jnp.dot(h, wa_ref[...], preferred_element_type=F32)
        u = jnp.dot(h, wu_ref[...], preferred_element_type=F32)
        t = (_silu(a) * u).astype(BF16)
        return jnp.dot(t, wo_ref[...], preferred_element_type=F32)

    def post_norm(acc):
        y = alpha * res_scr[...] + (0.5 * (1.0 + m_ref[2])) * acc.reshape(bb, ts, d)
        o_ref[...] = _layer_norm(y, g_ref[...], b_ref[...])

    @pl.when(j == 0)
    def _():
        acc = chunk(modulated_input())
        if n_chunks == 1:
            post_norm(acc)
        else:
            acc_scr[...] = acc

    if n_chunks > 2:
        @pl.when((j > 0) & (j < n_chunks - 1))
        def _():
            acc_scr[...] += chunk(h_scr[...])

    if n_chunks > 1:
        @pl.when(j == n_chunks - 1)
        def _():
            post_norm(acc_scr[...] + chunk(h_scr[...]))


def _row_tiling(x):
    nb, s, _ = x.shape
    if s >= FFN_ROWS:
        assert s % FFN_ROWS == 0
        return 1, FFN_ROWS
    assert s % 8 == 0 and FFN_ROWS % s == 0
    bb = min(nb, FFN_ROWS // s)
    assert nb % bb == 0
    return bb, s


def _ffn(x, m4, m_row0, sub, w_in, w_out, ln_g, ln_b, alpha, mixer_out=None, n_chunks=2):
    nb, s, d = x.shape
    bb, ts = _row_tiling(x)
    tpb = s // ts
    d_ff = w_out.shape[0]
    ck = d_ff // n_chunks
    assert ck * n_chunks == d_ff and ck % LANES == 0 and m_row0 % bb == 0
    grid = ((nb // bb) * tpb, n_chunks)
    xmap = lambda i, j: (i // tpb, i % tpb, 0)
    const2 = lambda i, j: (0, 0)

    def mspec(k):
        return pl.BlockSpec((3, bb, 1, d), lambda i, j: (k, m_row0 // bb + i // tpb, 0, 0))

    vec = pl.BlockSpec((1, d), const2)
    in_specs = [pl.BlockSpec((bb, ts, d), xmap)]
    args = [x]
    scratch = [pltpu.VMEM((bb * ts, d), BF16), pltpu.VMEM((bb * ts, d), F32)]
    if mixer_out is not None:
        o_moba, o_gla, w_mo, g_mo, b_mo = mixer_out
        in_specs += [pl.BlockSpec((bb, ts, D_MOBA), xmap), pl.BlockSpec((bb, ts, D_GLA), xmap),
                     mspec(1), pl.BlockSpec(w_mo.shape, const2), vec, vec]
        args += [o_moba, o_gla, m4, w_mo, g_mo.reshape(1, d), b_mo.reshape(1, d)]
        scratch.append(pltpu.VMEM((bb, ts, d), F32))
    in_specs += [mspec(sub),
                 pl.BlockSpec((d, ck), lambda i, j: (0, j)),
                 pl.BlockSpec((d, ck), lambda i, j: (0, j + n_chunks)),
                 pl.BlockSpec((ck, d), lambda i, j: (j, 0)),
                 vec, vec]
    args += [m4, w_in, w_in, w_out, ln_g.reshape(1, d), ln_b.reshape(1, d)]
    return pl.pallas_call(
        functools.partial(_ffn_kernel, alpha=alpha, fused_mixer_out=mixer_out is not None,
                          n_chunks=n_chunks),
        out_shape=jax.ShapeDtypeStruct(x.shape, F32),
        grid=grid,
        in_specs=in_specs,
        out_specs=pl.BlockSpec((bb, ts, d), xmap),
        scratch_shapes=scratch,
        compiler_params=_cparams(("parallel", "arbitrary")),
        name="ffn_postnorm",
    )(*args)


def _rope_table_kernel(o_ref, *, pos_base):
    tt = o_ref.shape[1]
    lane = lax.broadcasted_iota(jnp.int32, (1, LANES), 1)
    fi = (lane & (ROPE_HALF - 1)).astype(F32)
    inv = jnp.power(jnp.full((1, LANES), ROPE_THETA, F32), -fi / ROPE_HALF)
    row = lax.broadcasted_iota(jnp.int32, (tt, 1), 0)
    pos = (pos_base + pl.program_id(0) * tt + row).astype(F32)
    ang = pos * inv
    cos = jnp.cos(ang)
    sin = jnp.sin(ang)
    l64 = lane & (MOBA_HEAD_DIM - 1)
    o_ref[0] = jnp.where(l64 < ROPE_DIMS, cos, 1.0)
    o_ref[1] = jnp.where(l64 < ROPE_HALF, -sin, 0.0)
    o_ref[2] = jnp.where((l64 >= ROPE_HALF) & (l64 < ROPE_DIMS), sin, 0.0)


def _rope_tables(n_pos, tile, pos_base):
    return pl.pallas_call(
        functools.partial(_rope_table_kernel, pos_base=pos_base),
        out_shape=jax.ShapeDtypeStruct((3, n_pos, LANES), F32),
        grid=(n_pos // tile,),
        in_specs=[],
        out_specs=pl.BlockSpec((3, tile, LANES), lambda i: (0, i, 0)),
        compiler_params=_cparams(("arbitrary",)),
        name="rope_tables",
    )()


def _mixin_kernel(x_ref, m_ref, tab_ref, w_ref, wrg_ref, wgk_ref, bgk_ref,
                  qm_ref, km_ref, vm_ref, qg_ref, kg_ref, vg_ref, gg_ref, la_ref,
                  *, kv_transposed):
    bb, ts, d = x_ref.shape
    tm = bb * ts
    h = (x_ref[...] * (1.0 + m_ref[1]) + m_ref[0]).reshape(tm, d).astype(BF16)
    p = jnp.dot(h, w_ref[...], preferred_element_type=F32)
    c_tab, s_lo, s_hi = tab_ref[0], tab_ref[1], tab_ref[2]

    def rope(x):
        slabs = []
        for s in range(x.shape[1] // LANES):
            xs = x[:, s * LANES:(s + 1) * LANES]
            r = (xs.reshape(bb, ts, LANES) * c_tab
                 + pltpu.roll(xs, LANES - ROPE_HALF, 1).reshape(bb, ts, LANES) * s_lo
                 + pltpu.roll(xs, ROPE_HALF, 1).reshape(bb, ts, LANES) * s_hi)
            slabs.append(r.reshape(tm, LANES))
        return jnp.concatenate(slabs, axis=1)

    def store(ref, x):
        ref[...] = x.reshape(ref.shape)

    def store_kv(ref, x):
        if kv_transposed:
            ref[0] = x.T
        else:
            store(ref, x)

    o = 0
    store(qm_ref, rope(p[:, o:o + D_MOBA])); o += D_MOBA
    store_kv(km_ref, rope(p[:, o:o + D_MOBA])); o += D_MOBA
    store_kv(vm_ref, p[:, o:o + D_MOBA]); o += D_MOBA
    store(qg_ref, p[:, o:o + D_GLA_K] * (GLA_DK ** -0.5)); o += D_GLA_K
    store(kg_ref, p[:, o:o + D_GLA_K]); o += D_GLA_K
    store(vg_ref, p[:, o:o + D_GLA]); o += D_GLA
    store(gg_ref, p[:, o:o + D_GLA]); o += D_GLA

    rg = jnp.dot(h, wrg_ref[...], preferred_element_type=F32)
    z = _dot3(rg, wgk_ref[...]) + bgk_ref[...]
    log_sig = jnp.minimum(z, 0.0) - jnp.log1p(jnp.exp(-jnp.abs(z)))
    store(la_ref, log_sig / GLA_GATE_NORM)


def _mixer_in(x, m4, m_row0, w_main, w_rg, w_gk, b_gk, pos_base, kv_transposed):
    nb, s, d = x.shape
    bb, ts = _row_tiling(x)
    tpb = s // ts
    assert m_row0 % bb == 0 and (bb == 1 or not kv_transposed)
    grid = ((nb // bb) * tpb,)
    xmap = lambda i: (i // tpb, i % tpb, 0)
    tmap = lambda i: (i // tpb, 0, i % tpb)
    mmap = lambda i: (1, m_row0 // bb + i // tpb, 0, 0)
    const2 = lambda i: (0, 0)
    widths = (D_MOBA, D_MOBA, D_MOBA, D_GLA_K, D_GLA_K, D_GLA, D_GLA, D_GLA_K)
    out_shape = [jax.ShapeDtypeStruct((nb, s, w), F32) for w in widths]
    out_specs = [pl.BlockSpec((bb, ts, w), xmap) for w in widths]
    if kv_transposed:
        for n in (1, 2):
            out_shape[n] = jax.ShapeDtypeStruct((nb, D_MOBA, s), F32)
            out_specs[n] = pl.BlockSpec((1, D_MOBA, ts), tmap)
    tables = _rope_tables(s, ts, pos_base)
    return pl.pallas_call(
        functools.partial(_mixin_kernel, kv_transposed=kv_transposed),
        out_shape=out_shape,
        grid=grid,
        in_specs=[pl.BlockSpec((bb, ts, d), xmap),
                  pl.BlockSpec((3, bb, 1, d), mmap),
                  pl.BlockSpec((3, ts, LANES), lambda i: (0, i % tpb, 0)),
                  pl.BlockSpec(w_main.shape, const2),
                  pl.BlockSpec(w_rg.shape, const2),
                  pl.BlockSpec(w_gk.shape, const2),
                  pl.BlockSpec((1, D_GLA_K), const2)],
        out_specs=out_specs,
        compiler_params=_cparams(("parallel",)),
        name="mixer_in",
    )(x, m4, tables, w_main, w_rg, w_gk, b_gk.reshape(1, -1))


def _softmax_step(s, pmask, pv, m_scr, l_scr, acc_scr):
    m_old = m_scr[...]
    m_new = jnp.maximum(m_old, jnp.max(jnp.where(pmask, s, NEG_BIG), axis=1, keepdims=True))
    p = jnp.where(pmask, jnp.exp(s - m_new), 0.0)
    alpha = jnp.exp(m_old - m_new)
    l_scr[...] = alpha * l_scr[...] + jnp.sum(p, axis=1, keepdims=True)
    acc_scr[...] = alpha * acc_scr[...] + pv(p)
    m_scr[...] = m_new


def _moba_prompt_kernel(pt_ref, q_ref, kt_ref, vt_ref, qs_ref, *refs, pages_per_step):
    del pt_ref
    pages = refs[:pages_per_step]
    o_ref, sc_ref, g_ref, kmean_scr = refs[pages_per_step:]
    i = pl.program_id(2)
    blk = q_ref.shape[1]
    s_len = kt_ref.shape[2]
    n_blocks = s_len // blk
    nb8 = kmean_scr.shape[1]
    hd = MOBA_HEAD_DIM
    blk_shift = blk.bit_length() - 1
    blk_row = lax.broadcasted_iota(jnp.int32, (nb8, 1), 0)

    @pl.when(i == 0)
    def _():
        blk_of_key = lax.shift_right_logical(lax.broadcasted_iota(jnp.int32, (1, s_len), 1), blk_shift)
        pool = jnp.where(blk_row == blk_of_key, 1.0 / blk, 0.0).astype(BF16)
        for h in range(2):
            kth = kt_ref[0, h * hd:(h + 1) * hd, :]
            hi = kth.astype(BF16)
            r1 = kth - hi.astype(F32)
            mid = r1.astype(BF16)
            lo = (r1 - mid.astype(F32)).astype(BF16)
            kmean_scr[h] = _dot_nt(pool, hi) + (_dot_nt(pool, mid) + _dot_nt(pool, lo))

    row = lax.broadcasted_iota(jnp.int32, (blk, 1), 0)
    col = lax.broadcasted_iota(jnp.int32, (1, blk), 1)
    causal = col <= row

    def picked_blocks(qh, h, own):
        gt = jnp.where(blk_row < own, _dot3(kmean_scr[h], qh, nt=True), -jnp.inf)
        rank = jnp.zeros(gt.shape, F32)
        for m in range(own):
            gm = gt[m:m + 1, :]
            beats = (gm > gt) | ((gm == gt) & (m < blk_row))
            rank += jnp.where(beats, 1.0, 0.0)
        sel_t = jnp.where((blk_row < own) & (rank < MOBA_TOPK), 1.0, 0.0)
        return _pad_rows(sel_t, LANES).T

    def attend(own):
        q = q_ref[0]
        n = (own + 1) * blk
        outs = []
        for h in range(2):
            qh = q[:, h * hd:(h + 1) * hd]
            s = _dot((qh * (hd ** -0.5)).astype(BF16), kt_ref[0, h * hd:(h + 1) * hd, 0:n])
            if own <= MOBA_TOPK:
                parts = [s[:, 0:own * blk]] if own else []
            else:
                sel = picked_blocks(qh, h, own)
                parts = []
                for j in range(own):
                    picked = jnp.broadcast_to(sel[:, j:j + 1], (blk, blk)) > 0.5
                    parts.append(jnp.where(picked, s[:, j * blk:(j + 1) * blk], -jnp.inf))
            parts.append(jnp.where(causal, s[:, own * blk:n], -jnp.inf))
            sm = jnp.concatenate(parts, axis=1)
            p = jnp.exp(sm - jnp.max(sm, axis=1, keepdims=True))
            l = jnp.sum(p, axis=1, keepdims=True)
            outs.append(_dot_nt(p, vt_ref[0, h * hd:(h + 1) * hd, 0:n]) / l)
        return jnp.concatenate(outs, axis=1)

    def score_pages():
        qbd = (_block_diag_queries(qs_ref[0]) * (hd ** -0.5)).astype(BF16)
        for p in range(pages_per_step):
            sc_ref[0, :, p * PAGE_SIZE:(p + 1) * PAGE_SIZE] = _dot(qbd, pages[p][0])
        lane = lax.broadcasted_iota(jnp.int32, (1, LANES), 1)
        g = jnp.zeros((qbd.shape[0], LANES), F32)
        for j in range(pages_per_step * PAGE_SIZE // MOBA_BLOCK):
            bs = jnp.sum(sc_ref[0, :, j * MOBA_BLOCK:(j + 1) * MOBA_BLOCK], axis=1, keepdims=True)
            g = jnp.where(lane == j, bs, g)
        g_ref[0, 0] = g

    for own in range(n_blocks):
        @pl.when(i == own)
        def _(own=own):
            o_ref[0] = attend(own)
            score_pages()


def _moba_prompt(qm, kt, vt, q_new, k_pages, page_table):
    nb, s, _ = qm.shape
    blk = MOBA_BLOCK
    assert s % blk == 0 and s // blk <= LANES
    n_pairs = D_MOBA // LANES
    n_tiles = s // blk
    nb8 = -(-n_tiles // 8) * 8
    nb_new, t, _ = q_new.shape
    n_pages = page_table.shape[1]
    pps = n_pages // n_tiles
    assert nb_new == nb * n_pairs and pps * n_tiles == n_pages and (pps * PAGE_SIZE) % MOBA_BLOCK == 0
    rows = MOBA_HEADS * t
    keys = pps * PAGE_SIZE
    qmap = lambda b, hp, i, pt: (b, i, hp)
    kmap = lambda b, hp, i, pt: (b, hp, 0)

    def page_spec(p):
        return pl.BlockSpec((1, D_MOBA, PAGE_SIZE),
                            lambda b, hp, i, pt, p=p: (pt[b * n_pairs + hp, i * pps + p], 0, 0))

    return pl.pallas_call(
        functools.partial(_moba_prompt_kernel, pages_per_step=pps),
        out_shape=[jax.ShapeDtypeStruct((nb, s, D_MOBA), F32),
                   jax.ShapeDtypeStruct((nb_new, rows, n_pages * PAGE_SIZE), F32),
                   jax.ShapeDtypeStruct((nb_new, n_tiles, rows, LANES), F32)],
        grid_spec=pltpu.PrefetchScalarGridSpec(
            num_scalar_prefetch=1, grid=(nb, n_pairs, n_tiles),
            in_specs=[pl.BlockSpec((1, blk, LANES), qmap),
                      pl.BlockSpec((1, LANES, s), kmap),
                      pl.BlockSpec((1, LANES, s), kmap),
                      pl.BlockSpec((1, t, D_MOBA), lambda b, hp, i, pt: (b * n_pairs + hp, 0, 0))]
                     + [page_spec(p) for p in range(pps)],
            out_specs=[pl.BlockSpec((1, blk, LANES), qmap),
                       pl.BlockSpec((1, rows, keys), lambda b, hp, i, pt: (b * n_pairs + hp, 0, i)),
                       pl.BlockSpec((1, 1, rows, LANES), lambda b, hp, i, pt: (b * n_pairs + hp, i, 0, 0))],
            scratch_shapes=[pltpu.VMEM((2, nb8, MOBA_HEAD_DIM), F32)]),
        compiler_params=_cparams(("parallel", "parallel", "arbitrary")),
        name="moba_prompt",
    )(page_table, qm, kt, vt, q_new, *([k_pages] * pps))


def _gla_out(o, gg, ng):
    ms = jnp.mean(o * o, axis=1, keepdims=True)
    return o * lax.rsqrt(ms + RMS_EPS) * ng * _silu(gg)


def _gla_prompt_kernel(q_ref, k_ref, la_ref, v_ref, gg_ref, ng_ref, o_ref, s_ref, st_scr):
    c = GLA_CHUNK
    n_chunks = q_ref.shape[1] // c
    lane = lax.broadcasted_iota(jnp.int32, (1, LANES), 1)
    row = lax.broadcasted_iota(jnp.int32, (c, 1), 0)
    col = lax.broadcasted_iota(jnp.int32, (1, c), 1)
    tril = col <= row
    tril_bf = jnp.where(tril, 1.0, 0.0).astype(BF16)
    ng = ng_ref[...]
    st_scr[...] = jnp.zeros_like(st_scr)

    def body(ci, carry):
        r0 = pl.multiple_of(ci * c, c)
        for hp in range(GLA_HEADS // 2):
            sl = slice(hp * LANES, (hp + 1) * LANES)
            q = q_ref[0, pl.ds(r0, c), sl]
            k = k_ref[0, pl.ds(r0, c), sl]
            b = _dot_exact_lhs(tril_bf, la_ref[0, pl.ds(r0, c), sl])
            b_end = b[c - 1:c, :]
            q_dec = q * jnp.exp(b)
            k_inv = k * jnp.exp(-b)
            k_dec = k * jnp.exp(b_end - b)
            e_end = jnp.exp(b_end)
            for hh in range(2):
                h = 2 * hp + hh
                hm = (lane >= GLA_DK * hh) & (lane < GLA_DK * (hh + 1))
                qb = jnp.where(hm, q_dec, 0.0)
                vh = v_ref[0, pl.ds(r0, c), h * GLA_DV:(h + 1) * GLA_DV]
                att = jnp.where(tril, _dot_nt(qb, k_inv), 0.0)
                st = st_scr[h]
                o = _dot(att, vh) + _dot_nt(qb, st)
                st_scr[h] = st * e_end + _dot(vh.T, jnp.where(hm, k_dec, 0.0))
                gg = gg_ref[0, pl.ds(r0, c), h * GLA_DV:(h + 1) * GLA_DV]
                o_ref[0, pl.ds(r0, c), h * GLA_DV:(h + 1) * GLA_DV] = _gla_out(o, gg, ng)
        return carry

    lax.fori_loop(0, n_chunks, body, 0, unroll=2)
    for h in range(GLA_HEADS):
        hh = h % 2
        s_ref[0, 0, h] = st_scr[h].T[hh * GLA_DK:(hh + 1) * GLA_DK, :]


def _gla_prompt(qg, kg, la, vg, gg, norm_g):
    nb, s, _ = qg.shape
    assert s % GLA_CHUNK == 0
    map3 = lambda b: (b, 0, 0)
    return pl.pallas_call(
        _gla_prompt_kernel,
        out_shape=[jax.ShapeDtypeStruct((nb, s, D_GLA), F32),
                   jax.ShapeDtypeStruct((1, nb, GLA_HEADS, GLA_DK, GLA_DV), F32)],
        grid=(nb,),
        in_specs=[pl.BlockSpec((1, s, D_GLA_K), map3),
                  pl.BlockSpec((1, s, D_GLA_K), map3),
                  pl.BlockSpec((1, s, D_GLA_K), map3),
                  pl.BlockSpec((1, s, D_GLA), map3),
                  pl.BlockSpec((1, s, D_GLA), map3),
                  pl.BlockSpec((1, GLA_DV), lambda b: (0, 0))],
        out_specs=[pl.BlockSpec((1, s, D_GLA), map3),
                   pl.BlockSpec((1, 1, GLA_HEADS, GLA_DK, GLA_DV), lambda b: (0, b, 0, 0, 0))],
        scratch_shapes=[pltpu.VMEM((GLA_HEADS, GLA_DV, LANES), F32)],
        compiler_params=_cparams(("parallel",)),
        name="gla_prompt",
    )(qg, kg, la, vg, gg, norm_g.reshape(1, GLA_DV))


def _gla_sample_kernel(q_ref, k_ref, la_ref, v_ref, gg_ref, s0_ref, ng_ref, o_ref, s_ref):
    t = q_ref.shape[1]
    lane = lax.broadcasted_iota(jnp.int32, (1, LANES), 1)
    row = lax.broadcasted_iota(jnp.int32, (t, 1), 0)
    ng = ng_ref[...]
    eye = (lax.broadcasted_iota(jnp.int32, (GLA_DK, 1), 0)
           == lax.broadcasted_iota(jnp.int32, (1, GLA_DK), 1))
    zeros_half = jnp.zeros((GLA_DK, GLA_DV), F32)
    for hp in range(GLA_HEADS // 2):
        sl = slice(hp * LANES, (hp + 1) * LANES)
        q = q_ref[0, :, sl]
        k = k_ref[0, :, sl]
        b = la_ref[0, :, sl]
        sh = 1
        while sh < t:
            b = b + jnp.where(row >= sh, pltpu.roll(b, sh, 0), 0.0)
            sh *= 2
        b_end = b[t - 1:t, :]
        e_end = jnp.exp(b_end)
        q_dec = q * jnp.exp(b)
        k_inv = _pad_rows(k * jnp.exp(-b), LANES)
        k_dec = k * jnp.exp(b_end - b)
        for hh in range(2):
            h = 2 * hp + hh
            hm = (lane >= GLA_DK * hh) & (lane < GLA_DK * (hh + 1))
            qb = jnp.where(hm, q_dec, 0.0)
            vh = _pad_rows(v_ref[0, :, h * GLA_DV:(h + 1) * GLA_DV], LANES)
            s0 = s0_ref[0, h]
            s0_pad = jnp.concatenate([s0, zeros_half] if hh == 0 else [zeros_half, s0], axis=0)
            att = jnp.where(lane <= row, _dot_nt(qb, k_inv), 0.0)
            o = _dot(att, vh) + _dot(qb, s0_pad)
            gg = gg_ref[0, :, h * GLA_DV:(h + 1) * GLA_DV]
            o_ref[0, :, h * GLA_DV:(h + 1) * GLA_DV] = _gla_out(o, gg, ng)
            kd = _pad_rows(jnp.where(hm, k_dec, 0.0), LANES)
            upd = _dot(kd.T, vh)[hh * GLA_DK:(hh + 1) * GLA_DK, :]
            e_h = e_end[:, hh * GLA_DK:(hh + 1) * GLA_DK]
            diag = jnp.where(eye, jnp.broadcast_to(e_h, (GLA_DK, GLA_DK)), 0.0)
            s_ref[0, 0, h] = _dot3(diag, s0) + upd


def _gla_sample(qg, kg, la, vg, gg, s0, norm_g):
    nb, t, _ = qg.shape
    map3 = lambda b: (b, 0, 0)
    return pl.pallas_call(
        _gla_sample_kernel,
        out_shape=[jax.ShapeDtypeStruct((nb, t, D_GLA), F32),
                   jax.ShapeDtypeStruct((1, nb, GLA_HEADS, GLA_DK, GLA_DV), F32)],
        grid=(nb,),
        in_specs=[pl.BlockSpec((1, t, D_GLA_K), map3),
                  pl.BlockSpec((1, t, D_GLA_K), map3),
                  pl.BlockSpec((1, t, D_GLA_K), map3),
                  pl.BlockSpec((1, t, D_GLA), map3),
                  pl.BlockSpec((1, t, D_GLA), map3),
                  pl.BlockSpec((1, GLA_HEADS, GLA_DK, GLA_DV), lambda b: (b, 0, 0, 0)),
                  pl.BlockSpec((1, GLA_DV), lambda b: (0, 0))],
        out_specs=[pl.BlockSpec((1, t, D_GLA), map3),
                   pl.BlockSpec((1, 1, GLA_HEADS, GLA_DK, GLA_DV), lambda b: (0, b, 0, 0, 0))],
        compiler_params=_cparams(("parallel",)),
        name="gla_sample",
    )(qg, kg, la, vg, gg, s0, norm_g.reshape(1, GLA_DV))


def _block_diag_queries(q):
    lane = lax.broadcasted_iota(jnp.int32, (1, D_MOBA), 1)
    parts = [jnp.where((lane >= MOBA_HEAD_DIM * h) & (lane < MOBA_HEAD_DIM * (h + 1)), q, 0.0)
             for h in range(MOBA_HEADS)]
    return jnp.concatenate(parts, axis=0)


def _moba_attend_kernel(pt_ref, q_ref, g_ref, sc_ref, kn_ref, vn_ref, *refs, n_full):
    del pt_ref
    n = PAGES_PER_STEP
    pages, o_ref = refs[:n], refs[n]
    sel_scr, m_scr, l_scr, acc_scr = refs[n + 1:]
    s = pl.program_id(1)
    t = q_ref.shape[1]
    rows = MOBA_HEADS * t
    keys = n * PAGE_SIZE
    bps = keys // MOBA_BLOCK
    lane = lax.broadcasted_iota(jnp.int32, (1, LANES), 1)
    lane_f = lane.astype(F32)

    @pl.when(s == 0)
    def _():
        gbps = n_full // g_ref.shape[1]
        g = jnp.zeros((rows, LANES), F32)
        for st in range(g_ref.shape[1]):
            g = g + pltpu.roll(g_ref[0, st], st * gbps, 1)
        g = jnp.where(lane < n_full, g, -jnp.inf)
        sel = jnp.zeros((rows, LANES), F32)
        for _ in range(MOBA_TOPK):
            mx = jnp.max(g, axis=1, keepdims=True)
            first = jnp.min(jnp.where(g == mx, lane_f, float(LANES)), axis=1, keepdims=True)
            pick = lane_f == first
            sel = jnp.where(pick, 1.0, sel)
            g = jnp.where(pick, -jnp.inf, g)
        sel_scr[...] = sel
        m_scr[...] = jnp.full(m_scr.shape, NEG_BIG, F32)
        l_scr[...] = jnp.zeros_like(l_scr)
        acc_scr[...] = jnp.zeros_like(acc_scr)

    col = lax.broadcasted_iota(jnp.int32, (1, keys), 1)
    blk_of_col = lax.shift_right_logical(col, MOBA_BLOCK.bit_length() - 1) + s * bps
    expand = jnp.where(lax.broadcasted_iota(jnp.int32, (LANES, 1), 0) == blk_of_col, 1.0, 0.0)
    pmask = _dot(sel_scr[...], expand) > 0.5

    def pv(p):
        p = p.astype(BF16)
        out = jnp.zeros(acc_scr.shape, F32)
        for pg in range(n):
            out += _dot_nt(p[:, pg * PAGE_SIZE:(pg + 1) * PAGE_SIZE], pages[pg][0])
        return out

    _softmax_step(sc_ref[0], pmask, pv, m_scr, l_scr, acc_scr)

    @pl.when(s == pl.num_programs(1) - 1)
    def _():
        qbd = _block_diag_queries(q_ref[0]) * (MOBA_HEAD_DIM ** -0.5)
        row_t = lax.broadcasted_iota(jnp.int32, (rows, 1), 0) & (t - 1)
        s_own = _dot_nt(qbd, _pad_rows(kn_ref[0], LANES))
        v_own = _pad_rows(vn_ref[0], LANES)
        _softmax_step(s_own, lane <= row_t, lambda p: _dot(p, v_own), m_scr, l_scr, acc_scr)
        out = acc_scr[...] / l_scr[...]
        lane_w = lax.broadcasted_iota(jnp.int32, (1, D_MOBA), 1)
        o = jnp.zeros((t, D_MOBA), F32)
        for h in range(MOBA_HEADS):
            hm = (lane_w >= MOBA_HEAD_DIM * h) & (lane_w < MOBA_HEAD_DIM * (h + 1))
            o = o + jnp.where(hm, out[h * t:(h + 1) * t, :], 0.0)
        o_ref[0] = o


def _moba_sample(qm, km, vm, scores, gates, v_pages, page_table):
    nb, t, _ = qm.shape
    n_pages = page_table.shape[1]
    past = n_pages * PAGE_SIZE
    n_full = past // MOBA_BLOCK
    assert past % MOBA_BLOCK == 0 and MOBA_TOPK <= n_full <= LANES and t & (t - 1) == 0
    n = PAGES_PER_STEP
    assert n_pages % n == 0 and (n * PAGE_SIZE) % MOBA_BLOCK == 0 and n_full % gates.shape[1] == 0
    steps = n_pages // n
    keys = n * PAGE_SIZE
    rows = MOBA_HEADS * t

    def page_spec(p):
        return pl.BlockSpec((1, D_MOBA, PAGE_SIZE), lambda b, s, pt, p=p: (pt[b, s * n + p], 0, 0))

    qspec = pl.BlockSpec((1, t, D_MOBA), lambda b, s, pt: (b, 0, 0))
    return pl.pallas_call(
        functools.partial(_moba_attend_kernel, n_full=n_full),
        out_shape=jax.ShapeDtypeStruct((nb, t, D_MOBA), F32),
        grid_spec=pltpu.PrefetchScalarGridSpec(
            num_scalar_prefetch=1, grid=(nb, steps),
            in_specs=[qspec,
                      pl.BlockSpec((1,) + gates.shape[1:], lambda b, s, pt: (b, 0, 0, 0)),
                      pl.BlockSpec((1, rows, keys), lambda b, s, pt: (b, 0, s)),
                      qspec, qspec] + [page_spec(p) for p in range(n)],
            out_specs=qspec,
            scratch_shapes=[pltpu.VMEM((rows, LANES), F32), pltpu.VMEM((rows, 1), F32),
                            pltpu.VMEM((rows, 1), F32), pltpu.VMEM((rows, D_MOBA), F32)]),
        compiler_params=_cparams(("parallel", "arbitrary")),
        name="moba_sample_attend",
    )(page_table, qm, gates, scores, km, vm, *([v_pages] * n))


def kernel(x_prompt, x_sample, cache_k, cache_v, state_gla, page_table, c_prompt, c_sample, w_ada, b_ada, ln_g, ln_b, w_ffn1_in, w_ffn1_out, w_mix_in, w_gk_up, b_gk, gla_norm_g, w_mix_out, w_ffn2_in, w_ffn2_out):
    depth = w_ada.shape[0]
    assert depth == 1, "one decoder layer"
    d = x_prompt.shape[-1]
    alpha = (2.0 * depth) ** 0.25
    nb_p, s_p, _ = x_prompt.shape
    nb_s, t_s, _ = x_sample.shape
    past_len = page_table.shape[1] * PAGE_SIZE

    w1_in, w1_out = w_ffn1_in[0].astype(BF16), w_ffn1_out[0].astype(BF16)
    w2_in, w2_out = w_ffn2_in[0].astype(BF16), w_ffn2_out[0].astype(BF16)
    w_mix = w_mix_in[0]
    w_main = w_mix[:, :D_MIX_MAIN].astype(BF16)
    w_rg = jnp.pad(w_mix[:, D_MIX_MAIN:], ((0, 0), (0, LANES - GLA_GATE_RANK))).astype(BF16)
    w_gk = jnp.pad(w_gk_up[0], ((0, LANES - GLA_GATE_RANK), (0, 0)))
    w_mo = w_mix_out[0].astype(BF16)

    c_all = jnp.concatenate([c_sample, c_prompt], axis=0)
    m = _ada_modulation(c_all, w_ada[0], b_ada[0])
    m4 = m.reshape(3 * N_SUBLAYERS, c_all.shape[0], 1, d)

    def first_half(x, m_row0, pos_base, kv_transposed):
        x = _ffn(x, m4, m_row0, 0, w1_in, w1_out, ln_g[0, 0], ln_b[0, 0], alpha)
        return x, _mixer_in(x, m4, m_row0, w_main, w_rg, w_gk, b_gk[0], pos_base, kv_transposed)

    def second_half(x, m_row0, o_moba, o_gla):
        return _ffn(x, m4, m_row0, 2, w2_in, w2_out, ln_g[0, 2], ln_b[0, 2], alpha,
                    mixer_out=(o_moba, o_gla, w_mo, ln_g[0, 1], ln_b[0, 1]))

    k_pages = jnp.transpose(cache_k[0], (0, 2, 3, 1)).reshape(cache_k.shape[1], D_MOBA, PAGE_SIZE)
    v_pages = jnp.transpose(cache_v[0], (0, 2, 3, 1)).reshape(cache_v.shape[1], D_MOBA, PAGE_SIZE)

    xs, (qm_s, ks, vs, qg_s, kg_s, vg_s, gg_s, la_s) = first_half(x_sample, 0, past_len, False)
    xp, (qm_p, kp_t, vp_t, qg_p, kg_p, vg_p, gg_p, la_p) = first_half(x_prompt, nb_s, 0, True)
    o_moba_p, scores, gates = _moba_prompt(qm_p, kp_t, vp_t, qm_s, k_pages, page_table)
    o_gla_p, sp = _gla_prompt(qg_p, kg_p, la_p, vg_p, gg_p, gla_norm_g[0])
    yp = second_half(xp, nb_s, o_moba_p, o_gla_p)
    o_moba_s = _moba_sample(qm_s, ks, vs, scores, gates, v_pages, page_table)
    o_gla_s, ss = _gla_sample(qg_s, kg_s, la_s, vg_s, gg_s, state_gla[0], gla_norm_g[0])
    ys = second_half(xs, 0, o_moba_s, o_gla_s)

    def rows_major(a_t):
        a = a_t.reshape(nb_p, MOBA_HEADS, MOBA_HEAD_DIM, s_p)
        return jnp.transpose(a, (0, 3, 1, 2))[None]

    def heads(a):
        return a.reshape(1, nb_s, t_s, MOBA_HEADS, MOBA_HEAD_DIM)

    return (yp, ys, rows_major(kp_t), rows_major(vp_t), sp, heads(ks), heads(vs), ss)
```

```python
import functools

import jax
import jax.numpy as jnp
from jax import lax
from jax.experimental import pallas as pl
from jax.experimental.pallas import tpu as pltpu

F32 = jnp.float32
BF16 = jnp.bfloat16

PAGE_SIZE = 128
MOBA_HEADS = 8
MOBA_HEAD_DIM = 64
D_MOBA = MOBA_HEADS * MOBA_HEAD_DIM
MOBA_BLOCK = 256
MOBA_TOPK = 3
ROPE_THETA = 500000.0
ROPE_DIMS = MOBA_HEAD_DIM // 4
ROPE_HALF = ROPE_DIMS // 2
GLA_HEADS = 4
GLA_DK = 64
GLA_DV = 128
D_GLA_K = GLA_HEADS * GLA_DK
D_GLA = GLA_HEADS * GLA_DV
GLA_GATE_RANK = 16
GLA_GATE_NORM = 16.0
D_MIX_MAIN = 3 * D_MOBA + 2 * D_GLA_K + 2 * D_GLA
N_SUBLAYERS = 3
LN_EPS = 1e-5
RMS_EPS = 1e-6

LANES = 128
NEG_BIG = -1e30
VMEM_LIMIT = 56 * 1024 * 1024

FFN_ROWS = 512
GLA_CHUNK = 128
GLA_LEAF = 32
PAGES_PER_STEP = 64


def _cparams(sem):
    return pltpu.CompilerParams(dimension_semantics=sem, vmem_limit_bytes=VMEM_LIMIT)


def _dot(a, b):
    return jnp.dot(a.astype(BF16), b.astype(BF16), preferred_element_type=F32)


def _dot_nt(a, b):
    return lax.dot_general(a.astype(BF16), b.astype(BF16), (((1,), (1,)), ((), ())),
                           preferred_element_type=F32)


def _split2(x):
    hi = x.astype(BF16)
    lo = (x - hi.astype(F32)).astype(BF16)
    return hi, lo


def _dot3(a, b, nt=False):
    d = _dot_nt if nt else _dot
    ah, al = _split2(a)
    bh, bl = _split2(b)
    return d(ah, bh) + (d(ah, bl) + d(al, bh))


def _dot_exact_lhs(lhs_bf16, x):
    hi = x.astype(BF16)
    r1 = x - hi.astype(F32)
    mid = r1.astype(BF16)
    lo = (r1 - mid.astype(F32)).astype(BF16)
    f = functools.partial(jnp.dot, lhs_bf16, preferred_element_type=F32)
    return f(hi) + (f(mid) + f(lo))


def _silu(x):
    return x * jax.nn.sigmoid(x)


def _layer_norm(y, g, b):
    mu = jnp.mean(y, axis=-1, keepdims=True)
    yc = y - mu
    var = jnp.mean(yc * yc, axis=-1, keepdims=True)
    return yc * lax.rsqrt(var + LN_EPS) * g + b


def _pad_rows(x, rows):
    return jnp.concatenate([x, jnp.zeros((rows - x.shape[0], x.shape[1]), x.dtype)], axis=0)


def _ada_kernel(c_ref, w_ref, b_ref, o_ref):
    o_ref[0] = _dot3(_silu(c_ref[...]), w_ref[...]) + b_ref[...]


def _ada_modulation(c_all, w_ada, b_ada):
    nb, d = c_all.shape
    n_out = w_ada.shape[1] // d
    return pl.pallas_call(
        _ada_kernel,
        out_shape=jax.ShapeDtypeStruct((n_out, nb, d), F32),
        grid=(n_out,),
        in_specs=[pl.BlockSpec((nb, d), lambda n: (0, 0)),
                  pl.BlockSpec((d, d), lambda n: (0, n)),
                  pl.BlockSpec((1, d), lambda n: (0, n))],
        out_specs=pl.BlockSpec((1, nb, d), lambda n: (n, 0, 0)),
        compiler_params=_cparams(("arbitrary",)),
        name="ada_modulation",
    )(c_all, w_ada, b_ada.reshape(1, -1))


def _ffn_kernel(*refs, alpha, fused_mixer_out, n_chunks):
    if fused_mixer_out:
        x_ref, om_ref, og_ref, mm_ref, wmo_ref, gm_ref, bm_ref = refs[:7]
        m_ref, wa_ref, wu_ref, wo_ref, g_ref, b_ref, o_ref, h_scr, acc_scr, res_scr = refs[7:]
    else:
        x_ref, m_ref, wa_ref, wu_ref, wo_ref, g_ref, b_ref, o_ref, h_scr, acc_scr = refs
        res_scr = x_ref
    j = pl.program_id(1)
    bb, ts, d = x_ref.shape
    tm = bb * ts

    def modulated_input():
        x = x_ref[...]
        if fused_mixer_out:
            mix = (_dot(om_ref[...].reshape(tm, D_MOBA), wmo_ref[0:D_MOBA, :])
                   + _dot(og_ref[...].reshape(tm, D_GLA), wmo_ref[D_MOBA:D_MOBA + D_GLA, :]))
            x = _layer_norm(alpha * x + (1.0 + mm_ref[2]) * mix.reshape(bb, ts, d), gm_ref[...], bm_ref[...])
            res_scr[...] = x
        h = (x * (1.0 + m_ref[1]) + m_ref[0]).reshape(tm, d).astype(BF16)
        h_scr[...] = h
        return h

    def chunk(h):
        a = jnp.dot(h, wa_ref[...], preferred_element_type=F32)
        u = jnp.dot(h, wu_ref[...], preferred_element_type=F32)
        t = (_silu(a) * u).astype(BF16)
        return jnp.dot(t, wo_ref[...], preferred_element_type=F32)

    def post_norm(acc):
        y = alpha * res_scr[...] + (0.5 * (1.0 + m_ref[2])) * acc.reshape(bb, ts, d)
        o_ref[...] = _layer_norm(y, g_ref[...], b_ref[...])

    @pl.when(j == 0)
    def _():
        acc = chunk(modulated_input())
        if n_chunks == 1:
            post_norm(acc)
        else:
            acc_scr[...] = acc

    if n_chunks > 2:
        @pl.when((j > 0) & (j < n_chunks - 1))
        def _():
            acc_scr[...] += chunk(h_scr[...])

    if n_chunks > 1:
        @pl.when(j == n_chunks - 1)
        def _():
            post_norm(acc_scr[...] + chunk(h_scr[...]))


def _row_tiling(x):
    nb, s, _ = x.shape
    if s >= FFN_ROWS:
        assert s % FFN_ROWS == 0
        return 1, FFN_ROWS
    assert s % 8 == 0 and FFN_ROWS % s == 0
    bb = min(nb, FFN_ROWS // s)
    assert nb % bb == 0
    return bb, s


def _ffn(x, m4, m_row0, sub, w_in, w_out, ln_g, ln_b, alpha, mixer_out=None, n_chunks=2):
    nb, s, d = x.shape
    bb, ts = _row_tiling(x)
    tpb = s // ts
    d_ff = w_out.shape[0]
    ck = d_ff // n_chunks
    assert ck * n_chunks == d_ff and ck % LANES == 0 and m_row0 % bb == 0
    grid = ((nb // bb) * tpb, n_chunks)
    xmap = lambda i, j: (i // tpb, i % tpb, 0)
    const2 = lambda i, j: (0, 0)

    def mspec(k):
        return pl.BlockSpec((3, bb, 1, d), lambda i, j: (k, m_row0 // bb + i // tpb, 0, 0))

    vec = pl.BlockSpec((1, d), const2)
    in_specs = [pl.BlockSpec((bb, ts, d), xmap)]
    args = [x]
    scratch = [pltpu.VMEM((bb * ts, d), BF16), pltpu.VMEM((bb * ts, d), F32)]
    if mixer_out is not None:
        o_moba, o_gla, w_mo, g_mo, b_mo = mixer_out
        in_specs += [pl.BlockSpec((bb, ts, D_MOBA), xmap), pl.BlockSpec((bb, ts, D_GLA), xmap),
                     mspec(1), pl.BlockSpec(w_mo.shape, const2), vec, vec]
        args += [o_moba, o_gla, m4, w_mo, g_mo.reshape(1, d), b_mo.reshape(1, d)]
        scratch.append(pltpu.VMEM((bb, ts, d), F32))
    in_specs += [mspec(sub),
                 pl.BlockSpec((d, ck), lambda i, j: (0, j)),
                 pl.BlockSpec((d, ck), lambda i, j: (0, j + n_chunks)),
                 pl.BlockSpec((ck, d), lambda i, j: (j, 0)),
                 vec, vec]
    args += [m4, w_in, w_in, w_out, ln_g.reshape(1, d), ln_b.reshape(1, d)]
    return pl.pallas_call(
        functools.partial(_ffn_kernel, alpha=alpha, fused_mixer_out=mixer_out is not None,
                          n_chunks=n_chunks),
        out_shape=jax.ShapeDtypeStruct(x.shape, F32),
        grid=grid,
        in_specs=in_specs,
        out_specs=pl.BlockSpec((bb, ts, d), xmap),
        scratch_shapes=scratch,
        compiler_params=_cparams(("parallel", "arbitrary")),
        name="ffn_postnorm",
    )(*args)


def _rope_table_kernel(o_ref, *, pos_base):
    tt = o_ref.shape[1]
    lane = lax.broadcasted_iota(jnp.int32, (1, LANES), 1)
    fi = (lane & (ROPE_HALF - 1)).astype(F32)
    inv = jnp.power(jnp.full((1, LANES), ROPE_THETA, F32), -fi / ROPE_HALF)
    row = lax.broadcasted_iota(jnp.int32, (tt, 1), 0)
    pos = (pos_base + pl.program_id(0) * tt + row).astype(F32)
    ang = pos * inv
    cos = jnp.cos(ang)
    sin = jnp.sin(ang)
    l64 = lane & (MOBA_HEAD_DIM - 1)
    o_ref[0] = jnp.where(l64 < ROPE_DIMS, cos, 1.0)
    o_ref[1] = jnp.where(l64 < ROPE_HALF, -sin, 0.0)
    o_ref[2] = jnp.where((l64 >= ROPE_HALF) & (l64 < ROPE_DIMS), sin, 0.0)


def _rope_tables(n_pos, tile, pos_base):
    return pl.pallas_call(
        functools.partial(_rope_table_kernel, pos_base=pos_base),
        out_shape=jax.ShapeDtypeStruct((3, n_pos, LANES), F32),
        grid=(n_pos // tile,),
        in_specs=[],
        out_specs=pl.BlockSpec((3, tile, LANES), lambda i: (0, i, 0)),
        compiler_params=_cparams(("arbitrary",)),
        name="rope_tables",
    )()


def _mixin_kernel(x_ref, m_ref, tab_ref, w_ref, wrg_ref, wgk_ref, bgk_ref,
                  qm_ref, km_ref, vm_ref, qg_ref, kg_ref, vg_ref, gg_ref, la_ref,
                  *, kv_transposed):
    bb, ts, d = x_ref.shape
    tm = bb * ts
    h = (x_ref[...] * (1.0 + m_ref[1]) + m_ref[0]).reshape(tm, d).astype(BF16)
    p = jnp.dot(h, w_ref[...], preferred_element_type=F32)
    c_tab, s_lo, s_hi = tab_ref[0], tab_ref[1], tab_ref[2]

    def rope(x):
        slabs = []
        for s in range(x.shape[1] // LANES):
            xs = x[:, s * LANES:(s + 1) * LANES]
            r = (xs.reshape(bb, ts, LANES) * c_tab
                 + pltpu.roll(xs, LANES - ROPE_HALF, 1).reshape(bb, ts, LANES) * s_lo
                 + pltpu.roll(xs, ROPE_HALF, 1).reshape(bb, ts, LANES) * s_hi)
            slabs.append(r.reshape(tm, LANES))
        return jnp.concatenate(slabs, axis=1)

    def store(ref, x):
        ref[...] = x.reshape(ref.shape)

    def store_kv(ref, x):
        if kv_transposed:
            ref[0] = x.T
        else:
            store(ref, x)

    o = 0
    store(qm_ref, rope(p[:, o:o + D_MOBA])); o += D_MOBA
    store_kv(km_ref, rope(p[:, o:o + D_MOBA])); o += D_MOBA
    store_kv(vm_ref, p[:, o:o + D_MOBA]); o += D_MOBA
    store(qg_ref, p[:, o:o + D_GLA_K] * (GLA_DK ** -0.5)); o += D_GLA_K
    store(kg_ref, p[:, o:o + D_GLA_K]); o += D_GLA_K
    store(vg_ref, p[:, o:o + D_GLA]); o += D_GLA
    store(gg_ref, p[:, o:o + D_GLA]); o += D_GLA

    rg = jnp.dot(h, wrg_ref[...], preferred_element_type=F32)
    z = _dot3(rg, wgk_ref[...]) + bgk_ref[...]
    log_sig = jnp.minimum(z, 0.0) - jnp.log1p(jnp.exp(-jnp.abs(z)))
    store(la_ref, log_sig / GLA_GATE_NORM)


def _mixer_in(x, m4, m_row0, w_main, w_rg, w_gk, b_gk, pos_base, kv_transposed):
    nb, s, d = x.shape
    bb, ts = _row_tiling(x)
    tpb = s // ts
    assert m_row0 % bb == 0 and (bb == 1 or not kv_transposed)
    grid = ((nb // bb) * tpb,)
    xmap = lambda i: (i // tpb, i % tpb, 0)
    tmap = lambda i: (i // tpb, 0, i % tpb)
    mmap = lambda i: (1, m_row0 // bb + i // tpb, 0, 0)
    const2 = lambda i: (0, 0)
    widths = (D_MOBA, D_MOBA, D_MOBA, D_GLA_K, D_GLA_K, D_GLA, D_GLA, D_GLA_K)
    out_shape = [jax.ShapeDtypeStruct((nb, s, w), F32) for w in widths]
    out_specs = [pl.BlockSpec((bb, ts, w), xmap) for w in widths]
    if kv_transposed:
        for n in (1, 2):
            out_shape[n] = jax.ShapeDtypeStruct((nb, D_MOBA, s), F32)
            out_specs[n] = pl.BlockSpec((1, D_MOBA, ts), tmap)
    tables = _rope_tables(s, ts, pos_base)
    return pl.pallas_call(
        functools.partial(_mixin_kernel, kv_transposed=kv_transposed),
        out_shape=out_shape,
        grid=grid,
        in_specs=[pl.BlockSpec((bb, ts, d), xmap),
                  pl.BlockSpec((3, bb, 1, d), mmap),
                  pl.BlockSpec((3, ts, LANES), lambda i: (0, i % tpb, 0)),
                  pl.BlockSpec(w_main.shape, const2),
                  pl.BlockSpec(w_rg.shape, const2),
                  pl.BlockSpec(w_gk.shape, const2),
                  pl.BlockSpec((1, D_GLA_K), const2)],
        out_specs=out_specs,
        compiler_params=_cparams(("parallel",)),
        name="mixer_in",
    )(x, m4, tables, w_main, w_rg, w_gk, b_gk.reshape(1, -1))


def _softmax_step(s, pmask, pv, m_scr, l_scr, acc_scr):
    m_old = m_scr[...]
    m_new = jnp.maximum(m_old, jnp.max(jnp.where(pmask, s, NEG_BIG), axis=1, keepdims=True))
    p = jnp.where(pmask, jnp.exp(s - m_new), 0.0)
    alpha = jnp.exp(m_old - m_new)
    l_scr[...] = alpha * l_scr[...] + jnp.sum(p, axis=1, keepdims=True)
    acc_scr[...] = alpha * acc_scr[...] + pv(p)
    m_scr[...] = m_new


def _moba_prompt_kernel(pt_ref, q_ref, kt_ref, vt_ref, qs_ref, *refs, pages_per_step):
    del pt_ref
    pages = refs[:pages_per_step]
    o_ref, sc_ref, g_ref, kmean_scr = refs[pages_per_step:]
    i = pl.program_id(2)
    blk = q_ref.shape[1]
    s_len = kt_ref.shape[2]
    n_blocks = s_len // blk
    nb8 = kmean_scr.shape[1]
    hd = MOBA_HEAD_DIM
    blk_shift = blk.bit_length() - 1
    blk_row = lax.broadcasted_iota(jnp.int32, (nb8, 1), 0)

    @pl.when(i == 0)
    def _():
        blk_of_key = lax.shift_right_logical(lax.broadcasted_iota(jnp.int32, (1, s_len), 1), blk_shift)
        pool = jnp.where(blk_row == blk_of_key, 1.0 / blk, 0.0).astype(BF16)
        for h in range(2):
            kth = kt_ref[0, h * hd:(h + 1) * hd, :]
            hi = kth.astype(BF16)
            r1 = kth - hi.astype(F32)
            mid = r1.astype(BF16)
            lo = (r1 - mid.astype(F32)).astype(BF16)
            kmean_scr[h] = _dot_nt(pool, hi) + (_dot_nt(pool, mid) + _dot_nt(pool, lo))

    row = lax.broadcasted_iota(jnp.int32, (blk, 1), 0)
    col = lax.broadcasted_iota(jnp.int32, (1, blk), 1)
    causal = col <= row

    def picked_blocks(qh, h, own):
        gt = jnp.where(blk_row < own, _dot3(kmean_scr[h], qh, nt=True), -jnp.inf)
        rank = jnp.zeros(gt.shape, F32)
        for m in range(own):
            gm = gt[m:m + 1, :]
            beats = (gm > gt) | ((gm == gt) & (m < blk_row))
            rank += jnp.where(beats, 1.0, 0.0)
        sel_t = jnp.where((blk_row < own) & (rank < MOBA_TOPK), 1.0, 0.0)
        return _pad_rows(sel_t, LANES).T

    def attend(own):
        q = q_ref[0]
        n = (own + 1) * blk
        outs = []
        for h in range(2):
            qh = q[:, h * hd:(h + 1) * hd]
            s = _dot((qh * (hd ** -0.5)).astype(BF16), kt_ref[0, h * hd:(h + 1) * hd, 0:n])
            if own <= MOBA_TOPK:
                parts = [s[:, 0:own * blk]] if own else []
            else:
                sel = picked_blocks(qh, h, own)
                parts = []
                for j in range(own):
                    picked = jnp.broadcast_to(sel[:, j:j + 1], (blk, blk)) > 0.5
                    parts.append(jnp.where(picked, s[:, j * blk:(j + 1) * blk], -jnp.inf))
            parts.append(jnp.where(causal, s[:, own * blk:n], -jnp.inf))
            sm = jnp.concatenate(parts, axis=1)
            p = jnp.exp(sm - jnp.max(sm, axis=1, keepdims=True))
            l = jnp.sum(p, axis=1, keepdims=True)
            outs.append(_dot_nt(p, vt_ref[0, h * hd:(h + 1) * hd, 0:n]) / l)
        return jnp.concatenate(outs, axis=1)

    def score_pages():
        qbd = (_block_diag_queries(qs_ref[0]) * (hd ** -0.5)).astype(BF16)
        for p in range(pages_per_step):
            sc_ref[0, :, p * PAGE_SIZE:(p + 1) * PAGE_SIZE] = _dot(qbd, pages[p][0])
        lane = lax.broadcasted_iota(jnp.int32, (1, LANES), 1)
        g = jnp.zeros((qbd.shape[0], LANES), F32)
        for j in range(pages_per_step * PAGE_SIZE // MOBA_BLOCK):
            bs = jnp.sum(sc_ref[0, :, j * MOBA_BLOCK:(j + 1) * MOBA_BLOCK], axis=1, keepdims=True)
            g = jnp.where(lane == j, bs, g)
        g_ref[0, 0] = g

    for own in range(n_blocks):
        @pl.when(i == own)
        def _(own=own):
            o_ref[0] = attend(own)
            score_pages()


def _moba_prompt(qm, kt, vt, q_new, k_pages, page_table):
    nb, s, _ = qm.shape
    blk = MOBA_BLOCK
    assert s % blk == 0 and s // blk <= LANES
    n_pairs = D_MOBA // LANES
    n_tiles = s // blk
    nb8 = -(-n_tiles // 8) * 8
    nb_new, t, _ = q_new.shape
    n_pages = page_table.shape[1]
    pps = n_pages // n_tiles
    assert nb_new == nb * n_pairs and pps * n_tiles == n_pages and (pps * PAGE_SIZE) % MOBA_BLOCK == 0
    rows = MOBA_HEADS * t
    keys = pps * PAGE_SIZE
    qmap = lambda b, hp, i, pt: (b, i, hp)
    kmap = lambda b, hp, i, pt: (b, hp, 0)

    def page_spec(p):
        return pl.BlockSpec((1, D_MOBA, PAGE_SIZE),
                            lambda b, hp, i, pt, p=p: (pt[b * n_pairs + hp, i * pps + p], 0, 0))

    return pl.pallas_call(
        functools.partial(_moba_prompt_kernel, pages_per_step=pps),
        out_shape=[jax.ShapeDtypeStruct((nb, s, D_MOBA), F32),
                   jax.ShapeDtypeStruct((nb_new, rows, n_pages * PAGE_SIZE), F32),
                   jax.ShapeDtypeStruct((nb_new, n_tiles, rows, LANES), F32)],
        grid_spec=pltpu.PrefetchScalarGridSpec(
            num_scalar_prefetch=1, grid=(nb, n_pairs, n_tiles),
            in_specs=[pl.BlockSpec((1, blk, LANES), qmap),
                      pl.BlockSpec((1, LANES, s), kmap),
                      pl.BlockSpec((1, LANES, s), kmap),
                      pl.BlockSpec((1, t, D_MOBA), lambda b, hp, i, pt: (b * n_pairs + hp, 0, 0))]
                     + [page_spec(p) for p in range(pps)],
            out_specs=[pl.BlockSpec((1, blk, LANES), qmap),
                       pl.BlockSpec((1, rows, keys), lambda b, hp, i, pt: (b * n_pairs + hp, 0, i)),
                       pl.BlockSpec((1, 1, rows, LANES), lambda b, hp, i, pt: (b * n_pairs + hp, i, 0, 0))],
            scratch_shapes=[pltpu.VMEM((2, nb8, MOBA_HEAD_DIM), F32)]),
        compiler_params=_cparams(("parallel", "parallel", "arbitrary")),
        name="moba_prompt",
    )(page_table, qm, kt, vt, q_new, *([k_pages] * pps))


def _gla_out(o, gg, ng):
    ms = jnp.mean(o * o, axis=1, keepdims=True)
    return o * lax.rsqrt(ms + RMS_EPS) * ng * _silu(gg)


def _gla_prompt_kernel(q_ref, k_ref, la_ref, v_ref, gg_ref, ng_ref, o_ref, s_ref, st_scr):
    c = GLA_CHUNK
    n_chunks = q_ref.shape[1] // c
    lane = lax.broadcasted_iota(jnp.int32, (1, LANES), 1)
    row = lax.broadcasted_iota(jnp.int32, (c, 1), 0)
    col = lax.broadcasted_iota(jnp.int32, (1, c), 1)
    tril_bf = jnp.where(col <= row, 1.0, 0.0).astype(BF16)
    ng = ng_ref[...]
    st_scr[...] = jnp.zeros_like(st_scr)

    halves = []
    half = c // 2
    while half >= GLA_LEAF:
        halves.append(half)
        half //= 2
    def same_block(size):
        sh = size.bit_length() - 1
        return lax.shift_right_logical(row, sh) == lax.shift_right_logical(col, sh)

    pair_masks = [same_block(2 * hf) & ((row & hf) != 0) & ((col & hf) == 0) for hf in halves]
    leaf_mask = same_block(GLA_LEAF) & (col <= row)

    def rows_of(b, size, pick):
        parts = []
        for n in range(c // size):
            r = pick(n)
            src = b[r:r + 1, :] if r >= 0 else jnp.zeros((1, LANES), F32)
            parts.append(jnp.broadcast_to(src, (size, LANES)))
        return jnp.concatenate(parts, axis=0)

    def body(ci, carry):
        r0 = pl.multiple_of(ci * c, c)
        for hp in range(GLA_HEADS // 2):
            sl = slice(hp * LANES, (hp + 1) * LANES)
            q = q_ref[0, pl.ds(r0, c), sl]
            k = k_ref[0, pl.ds(r0, c), sl]
            b = _dot_exact_lhs(tril_bf, la_ref[0, pl.ds(r0, c), sl])
            b_end = b[c - 1:c, :]
            q_dec = q * jnp.exp(b)
            k_dec = k * jnp.exp(b_end - b)
            e_end = jnp.exp(b_end)
            level_qk = []
            for hf in halves:
                u = jnp.exp(-jnp.abs(b - rows_of(b, 2 * hf, lambda n: n * 2 * hf + hf - 1)))
                level_qk.append((q * u, k * u))
            b_leaf = b - rows_of(b, GLA_LEAF, lambda n: n * GLA_LEAF - 1)
            q_leaf, k_leaf = q * jnp.exp(b_leaf), k * jnp.exp(-b_leaf)
            for hh in range(2):
                h = 2 * hp + hh
                hm = (lane >= GLA_DK * hh) & (lane < GLA_DK * (hh + 1))
                qb = jnp.where(hm, q_dec, 0.0)
                vh = v_ref[0, pl.ds(r0, c), h * GLA_DV:(h + 1) * GLA_DV]
                att = jnp.where(leaf_mask, _dot_nt(jnp.where(hm, q_leaf, 0.0), k_leaf), 0.0)
                for mask, (ql, kl) in zip(pair_masks, level_qk):
                    att = jnp.where(mask, _dot_nt(jnp.where(hm, ql, 0.0), kl), att)
                st = st_scr[h]
                o = _dot(att, vh) + _dot_nt(qb, st)
                st_scr[h] = st * e_end + _dot(vh.T, jnp.where(hm, k_dec, 0.0))
                gg = gg_ref[0, pl.ds(r0, c), h * GLA_DV:(h + 1) * GLA_DV]
                o_ref[0, pl.ds(r0, c), h * GLA_DV:(h + 1) * GLA_DV] = _gla_out(o, gg, ng)
        return carry

    lax.fori_loop(0, n_chunks, body, 0, unroll=2)
    for h in range(GLA_HEADS):
        hh = h % 2
        s_ref[0, 0, h] = st_scr[h].T[hh * GLA_DK:(hh + 1) * GLA_DK, :]


def _gla_prompt(qg, kg, la, vg, gg, norm_g):
    nb, s, _ = qg.shape
    assert s % GLA_CHUNK == 0
    map3 = lambda b: (b, 0, 0)
    return pl.pallas_call(
        _gla_prompt_kernel,
        out_shape=[jax.ShapeDtypeStruct((nb, s, D_GLA), F32),
                   jax.ShapeDtypeStruct((1, nb, GLA_HEADS, GLA_DK, GLA_DV), F32)],
        grid=(nb,),
        in_specs=[pl.BlockSpec((1, s, D_GLA_K), map3),
                  pl.BlockSpec((1, s, D_GLA_K), map3),
                  pl.BlockSpec((1, s, D_GLA_K), map3),
                  pl.BlockSpec((1, s, D_GLA), map3),
                  pl.BlockSpec((1, s, D_GLA), map3),
                  pl.BlockSpec((1, GLA_DV), lambda b: (0, 0))],
        out_specs=[pl.BlockSpec((1, s, D_GLA), map3),
                   pl.BlockSpec((1, 1, GLA_HEADS, GLA_DK, GLA_DV), lambda b: (0, b, 0, 0, 0))],
        scratch_shapes=[pltpu.VMEM((GLA_HEADS, GLA_DV, LANES), F32)],
        compiler_params=_cparams(("parallel",)),
        name="gla_prompt",
    )(qg, kg, la, vg, gg, norm_g.reshape(1, GLA_DV))


def _gla_sample_kernel(q_ref, k_ref, la_ref, v_ref, gg_ref, s0_ref, ng_ref, o_ref, s_ref):
    t = q_ref.shape[1]
    lane = lax.broadcasted_iota(jnp.int32, (1, LANES), 1)
    row = lax.broadcasted_iota(jnp.int32, (t, 1), 0)
    ng = ng_ref[...]
    eye = (lax.broadcasted_iota(jnp.int32, (GLA_DK, 1), 0)
           == lax.broadcasted_iota(jnp.int32, (1, GLA_DK), 1))
    zeros_half = jnp.zeros((GLA_DK, GLA_DV), F32)
    for hp in range(GLA_HEADS // 2):
        sl = slice(hp * LANES, (hp + 1) * LANES)
        q = q_ref[0, :, sl]
        k = k_ref[0, :, sl]
        b = la_ref[0, :, sl]
        sh = 1
        while sh < t:
            b = b + jnp.where(row >= sh, pltpu.roll(b, sh, 0), 0.0)
            sh *= 2
        b_end = b[t - 1:t, :]
        e_end = jnp.exp(b_end)
        q_dec = q * jnp.exp(b)
        k_inv = _pad_rows(k * jnp.exp(-b), LANES)
        k_dec = k * jnp.exp(b_end - b)
        for hh in range(2):
            h = 2 * hp + hh
            hm = (lane >= GLA_DK * hh) & (lane < GLA_DK * (hh + 1))
            qb = jnp.where(hm, q_dec, 0.0)
            vh = _pad_rows(v_ref[0, :, h * GLA_DV:(h + 1) * GLA_DV], LANES)
            s0 = s0_ref[0, h]
            s0_pad = jnp.concatenate([s0, zeros_half] if hh == 0 else [zeros_half, s0], axis=0)
            att = jnp.where(lane <= row, _dot_nt(qb, k_inv), 0.0)
            o = _dot(att, vh) + _dot(qb, s0_pad)
            gg = gg_ref[0, :, h * GLA_DV:(h + 1) * GLA_DV]
            o_ref[0, :, h * GLA_DV:(h + 1) * GLA_DV] = _gla_out(o, gg, ng)
            kd = _pad_rows(jnp.where(hm, k_dec, 0.0), LANES)
            upd = _dot(kd.T, vh)[hh * GLA_DK:(hh + 1) * GLA_DK, :]
            e_h = e_end[:, hh * GLA_DK:(hh + 1) * GLA_DK]
            diag = jnp.where(eye, jnp.broadcast_to(e_h, (GLA_DK, GLA_DK)), 0.0)
            s_ref[0, 0, h] = _dot3(diag, s0) + upd


def _gla_sample(qg, kg, la, vg, gg, s0, norm_g):
    nb, t, _ = qg.shape
    map3 = lambda b: (b, 0, 0)
    return pl.pallas_call(
        _gla_sample_kernel,
        out_shape=[jax.ShapeDtypeStruct((nb, t, D_GLA), F32),
                   jax.ShapeDtypeStruct((1, nb, GLA_HEADS, GLA_DK, GLA_DV), F32)],
        grid=(nb,),
        in_specs=[pl.BlockSpec((1, t, D_GLA_K), map3),
                  pl.BlockSpec((1, t, D_GLA_K), map3),
                  pl.BlockSpec((1, t, D_GLA_K), map3),
                  pl.BlockSpec((1, t, D_GLA), map3),
                  pl.BlockSpec((1, t, D_GLA), map3),
                  pl.BlockSpec((1, GLA_HEADS, GLA_DK, GLA_DV), lambda b: (b, 0, 0, 0)),
                  pl.BlockSpec((1, GLA_DV), lambda b: (0, 0))],
        out_specs=[pl.BlockSpec((1, t, D_GLA), map3),
                   pl.BlockSpec((1, 1, GLA_HEADS, GLA_DK, GLA_DV), lambda b: (0, b, 0, 0, 0))],
        compiler_params=_cparams(("parallel",)),
        name="gla_sample",
    )(qg, kg, la, vg, gg, s0, norm_g.reshape(1, GLA_DV))


def _block_diag_queries(q):
    lane = lax.broadcasted_iota(jnp.int32, (1, D_MOBA), 1)
    parts = [jnp.where((lane >= MOBA_HEAD_DIM * h) & (lane < MOBA_HEAD_DIM * (h + 1)), q, 0.0)
             for h in range(MOBA_HEADS)]
    return jnp.concatenate(parts, axis=0)


def _moba_attend_kernel(pt_ref, q_ref, g_ref, sc_ref, kn_ref, vn_ref, *refs, n_full):
    del pt_ref
    n = PAGES_PER_STEP
    pages, o_ref = refs[:n], refs[n]
    sel_scr, m_scr, l_scr, acc_scr = refs[n + 1:]
    s = pl.program_id(1)
    t = q_ref.shape[1]
    rows = MOBA_HEADS * t
    keys = n * PAGE_SIZE
    bps = keys // MOBA_BLOCK
    lane = lax.broadcasted_iota(jnp.int32, (1, LANES), 1)
    lane_f = lane.astype(F32)

    @pl.when(s == 0)
    def _():
        gbps = n_full // g_ref.shape[1]
        g = jnp.zeros((rows, LANES), F32)
        for st in range(g_ref.shape[1]):
            g = g + pltpu.roll(g_ref[0, st], st * gbps, 1)
        g = jnp.where(lane < n_full, g, -jnp.inf)
        sel = jnp.zeros((rows, LANES), F32)
        for _ in range(MOBA_TOPK):
            mx = jnp.max(g, axis=1, keepdims=True)
            first = jnp.min(jnp.where(g == mx, lane_f, float(LANES)), axis=1, keepdims=True)
            pick = lane_f == first
            sel = jnp.where(pick, 1.0, sel)
            g = jnp.where(pick, -jnp.inf, g)
        sel_scr[...] = sel
        m_scr[...] = jnp.full(m_scr.shape, NEG_BIG, F32)
        l_scr[...] = jnp.zeros_like(l_scr)
        acc_scr[...] = jnp.zeros_like(acc_scr)

    col = lax.broadcasted_iota(jnp.int32, (1, keys), 1)
    blk_of_col = lax.shift_right_logical(col, MOBA_BLOCK.bit_length() - 1) + s * bps
    expand = jnp.where(lax.broadcasted_iota(jnp.int32, (LANES, 1), 0) == blk_of_col, 1.0, 0.0)
    pmask = _dot(sel_scr[...], expand) > 0.5

    def pv(p):
        p = p.astype(BF16)
        out = jnp.zeros(acc_scr.shape, F32)
        for pg in range(n):
            out += _dot_nt(p[:, pg * PAGE_SIZE:(pg + 1) * PAGE_SIZE], pages[pg][0])
        return out

    _softmax_step(sc_ref[0], pmask, pv, m_scr, l_scr, acc_scr)

    @pl.when(s == pl.num_programs(1) - 1)
    def _():
        qbd = _block_diag_queries(q_ref[0]) * (MOBA_HEAD_DIM ** -0.5)
        row_t = lax.broadcasted_iota(jnp.int32, (rows, 1), 0) & (t - 1)
        s_own = _dot_nt(qbd, _pad_rows(kn_ref[0], LANES))
        v_own = _pad_rows(vn_ref[0], LANES)
        _softmax_step(s_own, lane <= row_t, lambda p: _dot(p, v_own), m_scr, l_scr, acc_scr)
        out = acc_scr[...] / l_scr[...]
        lane_w = lax.broadcasted_iota(jnp.int32, (1, D_MOBA), 1)
        o = jnp.zeros((t, D_MOBA), F32)
        for h in range(MOBA_HEADS):
            hm = (lane_w >= MOBA_HEAD_DIM * h) & (lane_w < MOBA_HEAD_DIM * (h + 1))
            o = o + jnp.where(hm, out[h * t:(h + 1) * t, :], 0.0)
        o_ref[0] = o


def _moba_sample(qm, km, vm, scores, gates, v_pages, page_table):
    nb, t, _ = qm.shape
    n_pages = page_table.shape[1]
    past = n_pages * PAGE_SIZE
    n_full = past // MOBA_BLOCK
    assert past % MOBA_BLOCK == 0 and MOBA_TOPK <= n_full <= LANES and t & (t - 1) == 0
    n = PAGES_PER_STEP
    assert n_pages % n == 0 and (n * PAGE_SIZE) % MOBA_BLOCK == 0 and n_full % gates.shape[1] == 0
    steps = n_pages // n
    keys = n * PAGE_SIZE
    rows = MOBA_HEADS * t

    def page_spec(p):
        return pl.BlockSpec((1, D_MOBA, PAGE_SIZE), lambda b, s, pt, p=p: (pt[b, s * n + p], 0, 0))

    qspec = pl.BlockSpec((1, t, D_MOBA), lambda b, s, pt: (b, 0, 0))
    return pl.pallas_call(
        functools.partial(_moba_attend_kernel, n_full=n_full),
        out_shape=jax.ShapeDtypeStruct((nb, t, D_MOBA), F32),
        grid_spec=pltpu.PrefetchScalarGridSpec(
            num_scalar_prefetch=1, grid=(nb, steps),
            in_specs=[qspec,
                      pl.BlockSpec((1,) + gates.shape[1:], lambda b, s, pt: (b, 0, 0, 0)),
                      pl.BlockSpec((1, rows, keys), lambda b, s, pt: (b, 0, s)),
                      qspec, qspec] + [page_spec(p) for p in range(n)],
            out_specs=qspec,
            scratch_shapes=[pltpu.VMEM((rows, LANES), F32), pltpu.VMEM((rows, 1), F32),
                            pltpu.VMEM((rows, 1), F32), pltpu.VMEM((rows, D_MOBA), F32)]),
        compiler_params=_cparams(("parallel", "arbitrary")),
        name="moba_sample_attend",
    )(page_table, qm, gates, scores, km, vm, *([v_pages] * n))


def kernel(x_prompt, x_sample, cache_k, cache_v, state_gla, page_table, c_prompt, c_sample, w_ada, b_ada, ln_g, ln_b, w_ffn1_in, w_ffn1_out, w_mix_in, w_gk_up, b_gk, gla_norm_g, w_mix_out, w_ffn2_in, w_ffn2_out):
    depth = w_ada.shape[0]
    assert depth == 1, "one decoder layer"
    d = x_prompt.shape[-1]
    alpha = (2.0 * depth) ** 0.25
    nb_p, s_p, _ = x_prompt.shape
    nb_s, t_s, _ = x_sample.shape
    past_len = page_table.shape[1] * PAGE_SIZE

    w1_in, w1_out = w_ffn1_in[0].astype(BF16), w_ffn1_out[0].astype(BF16)
    w2_in, w2_out = w_ffn2_in[0].astype(BF16), w_ffn2_out[0].astype(BF16)
    w_mix = w_mix_in[0]
    w_main = w_mix[:, :D_MIX_MAIN].astype(BF16)
    w_rg = jnp.pad(w_mix[:, D_MIX_MAIN:], ((0, 0), (0, LANES - GLA_GATE_RANK))).astype(BF16)
    w_gk = jnp.pad(w_gk_up[0], ((0, LANES - GLA_GATE_RANK), (0, 0)))
    w_mo = w_mix_out[0].astype(BF16)

    c_all = jnp.concatenate([c_sample, c_prompt], axis=0)
    m = _ada_modulation(c_all, w_ada[0], b_ada[0])
    m4 = m.reshape(3 * N_SUBLAYERS, c_all.shape[0], 1, d)

    def first_half(x, m_row0, pos_base, kv_transposed):
        x = _ffn(x, m4, m_row0, 0, w1_in, w1_out, ln_g[0, 0], ln_b[0, 0], alpha)
        return x, _mixer_in(x, m4, m_row0, w_main, w_rg, w_gk, b_gk[0], pos_base, kv_transposed)

    def second_half(x, m_row0, o_moba, o_gla):
        return _ffn(x, m4, m_row0, 2, w2_in, w2_out, ln_g[0, 2], ln_b[0, 2], alpha,
                    mixer_out=(o_moba, o_gla, w_mo, ln_g[0, 1], ln_b[0, 1]))

    k_pages = jnp.transpose(cache_k[0], (0, 2, 3, 1)).reshape(cache_k.shape[1], D_MOBA, PAGE_SIZE)
    v_pages = jnp.transpose(cache_v[0], (0, 2, 3, 1)).reshape(cache_v.shape[1], D_MOBA, PAGE_SIZE)

    xs, (qm_s, ks, vs, qg_s, kg_s, vg_s, gg_s, la_s) = first_half(x_sample, 0, past_len, False)
    xp, (qm_p, kp_t, vp_t, qg_p, kg_p, vg_p, gg_p, la_p) = first_half(x_prompt, nb_s, 0, True)
    o_moba_p, scores, gates = _moba_prompt(qm_p, kp_t, vp_t, qm_s, k_pages, page_table)
    o_gla_p, sp = _gla_prompt(qg_p, kg_p, la_p, vg_p, gg_p, gla_norm_g[0])
    yp = second_half(xp, nb_s, o_moba_p, o_gla_p)
    o_moba_s = _moba_sample(qm_s, ks, vs, scores, gates, v_pages, page_table)
    o_gla_s, ss = _gla_sample(qg_s, kg_s, la_s, vg_s, gg_s, state_gla[0], gla_norm_g[0])
    ys = second_half(xs, 0, o_moba_s, o_gla_s)

    def rows_major(a_t):
        a = a_t.reshape(nb_p, MOBA_HEADS, MOBA_HEAD_DIM, s_p)
        return jnp.transpose(a, (0, 3, 1, 2))[None]

    def heads(a):
        return a.reshape(1, nb_s, t_s, MOBA_HEADS, MOBA_HEAD_DIM)

    return (yp, ys, rows_major(kp_t), rows_major(vp_t), sp, heads(ks), heads(vs), ss)
```

```python
import functools

import jax
import jax.numpy as jnp
from jax import lax
from jax.experimental import pallas as pl
from jax.experimental.pallas import tpu as pltpu

F32 = jnp.float32
BF16 = jnp.bfloat16

PAGE_SIZE = 128
MOBA_HEADS = 8
MOBA_HEAD_DIM = 64
D_MOBA = MOBA_HEADS * MOBA_HEAD_DIM
MOBA_BLOCK = 256
MOBA_TOPK = 3
ROPE_THETA = 500000.0
ROPE_DIMS = MOBA_HEAD_DIM // 4
ROPE_HALF = ROPE_DIMS // 2
GLA_HEADS = 4
GLA_DK = 64
GLA_DV = 128
D_GLA_K = GLA_HEADS * GLA_DK
D_GLA = GLA_HEADS * GLA_DV
GLA_GATE_RANK = 16
GLA_GATE_NORM = 16.0
D_MIX_MAIN = 3 * D_MOBA + 2 * D_GLA_K + 2 * D_GLA
N_SUBLAYERS = 3
LN_EPS = 1e-5
RMS_EPS = 1e-6

LANES = 128
NEG_BIG = -1e30
VMEM_LIMIT = 56 * 1024 * 1024

FFN_ROWS = 512
FFN_CHUNKS = 1
GLA_CHUNK = 128
GLA_LEAF = 32
PAGES_PER_STEP = 64


def _cparams(sem):
    return pltpu.CompilerParams(dimension_semantics=sem, vmem_limit_bytes=VMEM_LIMIT)


def _dot(a, b):
    return jnp.dot(a.astype(BF16), b.astype(BF16), preferred_element_type=F32)


def _dot_nt(a, b):
    return lax.dot_general(a.astype(BF16), b.astype(BF16), (((1,), (1,)), ((), ())),
                           preferred_element_type=F32)


def _split2(x):
    hi = x.astype(BF16)
    lo = (x - hi.astype(F32)).astype(BF16)
    return hi, lo


def _dot3(a, b, nt=False):
    d = _dot_nt if nt else _dot
    ah, al = _split2(a)
    bh, bl = _split2(b)
    return d(ah, bh) + (d(ah, bl) + d(al, bh))


def _dot_exact_lhs(lhs_bf16, x):
    hi = x.astype(BF16)
    r1 = x - hi.astype(F32)
    mid = r1.astype(BF16)
    lo = (r1 - mid.astype(F32)).astype(BF16)
    f = functools.partial(jnp.dot, lhs_bf16, preferred_element_type=F32)
    return f(hi) + (f(mid) + f(lo))


def _silu(x):
    return x * jax.nn.sigmoid(x)


def _layer_norm(y, g, b):
    mu = jnp.mean(y, axis=-1, keepdims=True)
    yc = y - mu
    var = jnp.mean(yc * yc, axis=-1, keepdims=True)
    return yc * lax.rsqrt(var + LN_EPS) * g + b


def _pad_rows(x, rows):
    return jnp.concatenate([x, jnp.zeros((rows - x.shape[0], x.shape[1]), x.dtype)], axis=0)


def _ada_kernel(c_ref, w_ref, b_ref, o_ref):
    o_ref[0] = _dot3(_silu(c_ref[...]), w_ref[...]) + b_ref[...]


def _ada_modulation(c_all, w_ada, b_ada):
    nb, d = c_all.shape
    n_out = w_ada.shape[1] // d
    return pl.pallas_call(
        _ada_kernel,
        out_shape=jax.ShapeDtypeStruct((n_out, nb, d), F32),
        grid=(n_out,),
        in_specs=[pl.BlockSpec((nb, d), lambda n: (0, 0)),
                  pl.BlockSpec((d, d), lambda n: (0, n)),
                  pl.BlockSpec((1, d), lambda n: (0, n))],
        out_specs=pl.BlockSpec((1, nb, d), lambda n: (n, 0, 0)),
        compiler_params=_cparams(("arbitrary",)),
        name="ada_modulation",
    )(c_all, w_ada, b_ada.reshape(1, -1))


def _ffn_kernel(*refs, alpha, fused_mixer_out, n_chunks):
    if fused_mixer_out:
        x_ref, om_ref, og_ref, mm_ref, wmo_ref, gm_ref, bm_ref = refs[:7]
        m_ref, wa_ref, wu_ref, wo_ref, g_ref, b_ref, o_ref, h_scr, acc_scr, res_scr = refs[7:]
    else:
        x_ref, m_ref, wa_ref, wu_ref, wo_ref, g_ref, b_ref, o_ref, h_scr, acc_scr = refs
        res_scr = x_ref
    j = pl.program_id(1)
    bb, ts, d = x_ref.shape
    tm = bb * ts

    def modulated_input():
        x = x_ref[...]
        if fused_mixer_out:
            mix = (_dot(om_ref[...].reshape(tm, D_MOBA), wmo_ref[0:D_MOBA, :])
                   + _dot(og_ref[...].reshape(tm, D_GLA), wmo_ref[D_MOBA:D_MOBA + D_GLA, :]))
            x = _layer_norm(alpha * x + (1.0 + mm_ref[2]) * mix.reshape(bb, ts, d), gm_ref[...], bm_ref[...])
            res_scr[...] = x
        h = (x * (1.0 + m_ref[1]) + m_ref[0]).reshape(tm, d).astype(BF16)
        h_scr[...] = h
        return h

    def chunk(h):
        a = jnp.dot(h, wa_ref[...], preferred_element_type=F32)
        u = jnp.dot(h, wu_ref[...], preferred_element_type=F32)
        t = (_silu(a) * u).astype(BF16)
        return jnp.dot(t, wo_ref[...], preferred_element_type=F32)

    def post_norm(acc):
        y = alpha * res_scr[...] + (0.5 * (1.0 + m_ref[2])) * acc.reshape(bb, ts, d)
        o_ref[...] = _layer_norm(y, g_ref[...], b_ref[...])

    @pl.when(j == 0)
    def _():
        acc = chunk(modulated_input())
        if n_chunks == 1:
            post_norm(acc)
        else:
            acc_scr[...] = acc

    if n_chunks > 2:
        @pl.when((j > 0) & (j < n_chunks - 1))
        def _():
            acc_scr[...] += chunk(h_scr[...])

    if n_chunks > 1:
        @pl.when(j == n_chunks - 1)
        def _():
            post_norm(acc_scr[...] + chunk(h_scr[...]))


def _row_tiling(x):
    nb, s, _ = x.shape
    if s >= FFN_ROWS:
        assert s % FFN_ROWS == 0
        return 1, FFN_ROWS
    assert s % 8 == 0 and FFN_ROWS % s == 0
    bb = min(nb, FFN_ROWS // s)
    assert nb % bb == 0
    return bb, s


def _ffn(x, m4, m_row0, sub, w_in, w_out, ln_g, ln_b, alpha, mixer_out=None, n_chunks=FFN_CHUNKS):
    nb, s, d = x.shape
    bb, ts = _row_tiling(x)
    tpb = s // ts
    d_ff = w_out.shape[0]
    ck = d_ff // n_chunks
    assert ck * n_chunks == d_ff and ck % LANES == 0 and m_row0 % bb == 0
    grid = ((nb // bb) * tpb, n_chunks)
    xmap = lambda i, j: (i // tpb, i % tpb, 0)
    const2 = lambda i, j: (0, 0)

    def mspec(k):
        return pl.BlockSpec((3, bb, 1, d), lambda i, j: (k, m_row0 // bb + i // tpb, 0, 0))

    vec = pl.BlockSpec((1, d), const2)
    in_specs = [pl.BlockSpec((bb, ts, d), xmap)]
    args = [x]
    scratch = [pltpu.VMEM((bb * ts, d), BF16), pltpu.VMEM((bb * ts, d), F32)]
    if mixer_out is not None:
        o_moba, o_gla, w_mo, g_mo, b_mo = mixer_out
        in_specs += [pl.BlockSpec((bb, ts, D_MOBA), xmap), pl.BlockSpec((bb, ts, D_GLA), xmap),
                     mspec(1), pl.BlockSpec(w_mo.shape, const2), vec, vec]
        args += [o_moba, o_gla, m4, w_mo, g_mo.reshape(1, d), b_mo.reshape(1, d)]
        scratch.append(pltpu.VMEM((bb, ts, d), F32))
    in_specs += [mspec(sub),
                 pl.BlockSpec((d, ck), lambda i, j: (0, j)),
                 pl.BlockSpec((d, ck), lambda i, j: (0, j + n_chunks)),
                 pl.BlockSpec((ck, d), lambda i, j: (j, 0)),
                 vec, vec]
    args += [m4, w_in, w_in, w_out, ln_g.reshape(1, d), ln_b.reshape(1, d)]
    return pl.pallas_call(
        functools.partial(_ffn_kernel, alpha=alpha, fused_mixer_out=mixer_out is not None,
                          n_chunks=n_chunks),
        out_shape=jax.ShapeDtypeStruct(x.shape, F32),
        grid=grid,
        in_specs=in_specs,
        out_specs=pl.BlockSpec((bb, ts, d), xmap),
        scratch_shapes=scratch,
        compiler_params=_cparams(("parallel", "arbitrary")),
        name="ffn_postnorm",
    )(*args)


def _rope_table_kernel(o_ref, *, pos_base):
    tt = o_ref.shape[1]
    lane = lax.broadcasted_iota(jnp.int32, (1, LANES), 1)
    fi = (lane & (ROPE_HALF - 1)).astype(F32)
    inv = jnp.power(jnp.full((1, LANES), ROPE_THETA, F32), -fi / ROPE_HALF)
    row = lax.broadcasted_iota(jnp.int32, (tt, 1), 0)
    pos = (pos_base + pl.program_id(0) * tt + row).astype(F32)
    ang = pos * inv
    cos = jnp.cos(ang)
    sin = jnp.sin(ang)
    l64 = lane & (MOBA_HEAD_DIM - 1)
    o_ref[0] = jnp.where(l64 < ROPE_DIMS, cos, 1.0)
    o_ref[1] = jnp.where(l64 < ROPE_HALF, -sin, 0.0)
    o_ref[2] = jnp.where((l64 >= ROPE_HALF) & (l64 < ROPE_DIMS), sin, 0.0)


def _rope_tables(n_pos, tile, pos_base):
    return pl.pallas_call(
        functools.partial(_rope_table_kernel, pos_base=pos_base),
        out_shape=jax.ShapeDtypeStruct((3, n_pos, LANES), F32),
        grid=(n_pos // tile,),
        in_specs=[],
        out_specs=pl.BlockSpec((3, tile, LANES), lambda i: (0, i, 0)),
        compiler_params=_cparams(("arbitrary",)),
        name="rope_tables",
    )()


def _mixin_kernel(x_ref, m_ref, tab_ref, w_ref, wrg_ref, wgk_ref, bgk_ref,
                  qm_ref, km_ref, vm_ref, qg_ref, kg_ref, vg_ref, gg_ref, la_ref,
                  *, kv_transposed):
    bb, ts, d = x_ref.shape
    tm = bb * ts
    h = (x_ref[...] * (1.0 + m_ref[1]) + m_ref[0]).reshape(tm, d).astype(BF16)
    p = jnp.dot(h, w_ref[...], preferred_element_type=F32)
    c_tab, s_lo, s_hi = tab_ref[0], tab_ref[1], tab_ref[2]

    def rope(x):
        slabs = []
        for s in range(x.shape[1] // LANES):
            xs = x[:, s * LANES:(s + 1) * LANES]
            r = (xs.reshape(bb, ts, LANES) * c_tab
                 + pltpu.roll(xs, LANES - ROPE_HALF, 1).reshape(bb, ts, LANES) * s_lo
                 + pltpu.roll(xs, ROPE_HALF, 1).reshape(bb, ts, LANES) * s_hi)
            slabs.append(r.reshape(tm, LANES))
        return jnp.concatenate(slabs, axis=1)

    def store(ref, x):
        ref[...] = x.reshape(ref.shape)

    def store_kv(ref, x):
        if kv_transposed:
            ref[0] = x.T
        else:
            store(ref, x)

    o = 0
    store(qm_ref, rope(p[:, o:o + D_MOBA])); o += D_MOBA
    store_kv(km_ref, rope(p[:, o:o + D_MOBA])); o += D_MOBA
    store_kv(vm_ref, p[:, o:o + D_MOBA]); o += D_MOBA
    store(qg_ref, p[:, o:o + D_GLA_K] * (GLA_DK ** -0.5)); o += D_GLA_K
    store(kg_ref, p[:, o:o + D_GLA_K]); o += D_GLA_K
    store(vg_ref, p[:, o:o + D_GLA]); o += D_GLA
    store(gg_ref, p[:, o:o + D_GLA]); o += D_GLA

    rg = jnp.dot(h, wrg_ref[...], preferred_element_type=F32)
    z = _dot3(rg, wgk_ref[...]) + bgk_ref[...]
    log_sig = jnp.minimum(z, 0.0) - jnp.log1p(jnp.exp(-jnp.abs(z)))
    store(la_ref, log_sig / GLA_GATE_NORM)


def _mixer_in(x, m4, m_row0, w_main, w_rg, w_gk, b_gk, pos_base, kv_transposed):
    nb, s, d = x.shape
    bb, ts = _row_tiling(x)
    tpb = s // ts
    assert m_row0 % bb == 0 and (bb == 1 or not kv_transposed)
    grid = ((nb // bb) * tpb,)
    xmap = lambda i: (i // tpb, i % tpb, 0)
    tmap = lambda i: (i // tpb, 0, i % tpb)
    mmap = lambda i: (1, m_row0 // bb + i // tpb, 0, 0)
    const2 = lambda i: (0, 0)
    widths = (D_MOBA, D_MOBA, D_MOBA, D_GLA_K, D_GLA_K, D_GLA, D_GLA, D_GLA_K)
    out_shape = [jax.ShapeDtypeStruct((nb, s, w), F32) for w in widths]
    out_specs = [pl.BlockSpec((bb, ts, w), xmap) for w in widths]
    if kv_transposed:
        for n in (1, 2):
            out_shape[n] = jax.ShapeDtypeStruct((nb, D_MOBA, s), F32)
            out_specs[n] = pl.BlockSpec((1, D_MOBA, ts), tmap)
    tables = _rope_tables(s, ts, pos_base)
    return pl.pallas_call(
        functools.partial(_mixin_kernel, kv_transposed=kv_transposed),
        out_shape=out_shape,
        grid=grid,
        in_specs=[pl.BlockSpec((bb, ts, d), xmap),
                  pl.BlockSpec((3, bb, 1, d), mmap),
                  pl.BlockSpec((3, ts, LANES), lambda i: (0, i % tpb, 0)),
                  pl.BlockSpec(w_main.shape, const2),
                  pl.BlockSpec(w_rg.shape, const2),
                  pl.BlockSpec(w_gk.shape, const2),
                  pl.BlockSpec((1, D_GLA_K), const2)],
        out_specs=out_specs,
        compiler_params=_cparams(("parallel",)),
        name="mixer_in",
    )(x, m4, tables, w_main, w_rg, w_gk, b_gk.reshape(1, -1))


def _softmax_step(s, pmask, pv, m_scr, l_scr, acc_scr):
    m_old = m_scr[...]
    m_new = jnp.maximum(m_old, jnp.max(jnp.where(pmask, s, NEG_BIG), axis=1, keepdims=True))
    p = jnp.where(pmask, jnp.exp(s - m_new), 0.0)
    alpha = jnp.exp(m_old - m_new)
    l_scr[...] = alpha * l_scr[...] + jnp.sum(p, axis=1, keepdims=True)
    acc_scr[...] = alpha * acc_scr[...] + pv(p)
    m_scr[...] = m_new


def _moba_prompt_kernel(pt_ref, q_ref, kt_ref, vt_ref, qs_ref, *refs, pages_per_step):
    del pt_ref
    pages = refs[:pages_per_step]
    o_ref, sc_ref, g_ref, kmean_scr = refs[pages_per_step:]
    i = pl.program_id(2)
    blk = q_ref.shape[1]
    s_len = kt_ref.shape[2]
    n_blocks = s_len // blk
    nb8 = kmean_scr.shape[1]
    hd = MOBA_HEAD_DIM
    blk_shift = blk.bit_length() - 1
    blk_row = lax.broadcasted_iota(jnp.int32, (nb8, 1), 0)

    @pl.when(i == 0)
    def _():
        blk_of_key = lax.shift_right_logical(lax.broadcasted_iota(jnp.int32, (1, s_len), 1), blk_shift)
        pool = jnp.where(blk_row == blk_of_key, 1.0 / blk, 0.0).astype(BF16)
        for h in range(2):
            kth = kt_ref[0, h * hd:(h + 1) * hd, :]
            hi = kth.astype(BF16)
            r1 = kth - hi.astype(F32)
            mid = r1.astype(BF16)
            lo = (r1 - mid.astype(F32)).astype(BF16)
            kmean_scr[h] = _dot_nt(pool, hi) + (_dot_nt(pool, mid) + _dot_nt(pool, lo))

    row = lax.broadcasted_iota(jnp.int32, (blk, 1), 0)
    col = lax.broadcasted_iota(jnp.int32, (1, blk), 1)
    causal = col <= row

    def picked_blocks(qh, h, own):
        gt = jnp.where(blk_row < own, _dot3(kmean_scr[h], qh, nt=True), -jnp.inf)
        rank = jnp.zeros(gt.shape, F32)
        for m in range(own):
            gm = gt[m:m + 1, :]
            beats = (gm > gt) | ((gm == gt) & (m < blk_row))
            rank += jnp.where(beats, 1.0, 0.0)
        sel_t = jnp.where((blk_row < own) & (rank < MOBA_TOPK), 1.0, 0.0)
        return _pad_rows(sel_t, LANES).T

    def attend(own):
        q = q_ref[0]
        n = (own + 1) * blk
        outs = []
        for h in range(2):
            qh = q[:, h * hd:(h + 1) * hd]
            s = _dot((qh * (hd ** -0.5)).astype(BF16), kt_ref[0, h * hd:(h + 1) * hd, 0:n])
            if own <= MOBA_TOPK:
                parts = [s[:, 0:own * blk]] if own else []
            else:
                sel = picked_blocks(qh, h, own)
                parts = []
                for j in range(own):
                    picked = jnp.broadcast_to(sel[:, j:j + 1], (blk, blk)) > 0.5
                    parts.append(jnp.where(picked, s[:, j * blk:(j + 1) * blk], -jnp.inf))
            parts.append(jnp.where(causal, s[:, own * blk:n], -jnp.inf))
            sm = jnp.concatenate(parts, axis=1)
            p = jnp.exp(sm - jnp.max(sm, axis=1, keepdims=True))
            l = jnp.sum(p, axis=1, keepdims=True)
            outs.append(_dot_nt(p, vt_ref[0, h * hd:(h + 1) * hd, 0:n]) / l)
        return jnp.concatenate(outs, axis=1)

    def score_pages():
        qbd = (_block_diag_queries(qs_ref[0]) * (hd ** -0.5)).astype(BF16)
        for p in range(pages_per_step):
            sc_ref[0, :, p * PAGE_SIZE:(p + 1) * PAGE_SIZE] = _dot(qbd, pages[p][0])
        lane = lax.broadcasted_iota(jnp.int32, (1, LANES), 1)
        g = jnp.zeros((qbd.shape[0], LANES), F32)
        for j in range(pages_per_step * PAGE_SIZE // MOBA_BLOCK):
            bs = jnp.sum(sc_ref[0, :, j * MOBA_BLOCK:(j + 1) * MOBA_BLOCK], axis=1, keepdims=True)
            g = jnp.where(lane == j, bs, g)
        g_ref[0, 0] = g

    for own in range(n_blocks):
        @pl.when(i == own)
        def _(own=own):
            o_ref[0] = attend(own)
            score_pages()


def _moba_prompt(qm, kt, vt, q_new, k_pages, page_table):
    nb, s, _ = qm.shape
    blk = MOBA_BLOCK
    assert s % blk == 0 and s // blk <= LANES
    n_pairs = D_MOBA // LANES
    n_tiles = s // blk
    nb8 = -(-n_tiles // 8) * 8
    nb_new, t, _ = q_new.shape
    n_pages = page_table.shape[1]
    pps = n_pages // n_tiles
    assert nb_new == nb * n_pairs and pps * n_tiles == n_pages and (pps * PAGE_SIZE) % MOBA_BLOCK == 0
    rows = MOBA_HEADS * t
    keys = pps * PAGE_SIZE
    qmap = lambda b, hp, i, pt: (b, i, hp)
    kmap = lambda b, hp, i, pt: (b, hp, 0)

    def page_spec(p):
        return pl.BlockSpec((1, D_MOBA, PAGE_SIZE),
                            lambda b, hp, i, pt, p=p: (pt[b * n_pairs + hp, i * pps + p], 0, 0))

    return pl.pallas_call(
        functools.partial(_moba_prompt_kernel, pages_per_step=pps),
        out_shape=[jax.ShapeDtypeStruct((nb, s, D_MOBA), F32),
                   jax.ShapeDtypeStruct((nb_new, rows, n_pages * PAGE_SIZE), F32),
                   jax.ShapeDtypeStruct((nb_new, n_tiles, rows, LANES), F32)],
        grid_spec=pltpu.PrefetchScalarGridSpec(
            num_scalar_prefetch=1, grid=(nb, n_pairs, n_tiles),
            in_specs=[pl.BlockSpec((1, blk, LANES), qmap),
                      pl.BlockSpec((1, LANES, s), kmap),
                      pl.BlockSpec((1, LANES, s), kmap),
                      pl.BlockSpec((1, t, D_MOBA), lambda b, hp, i, pt: (b * n_pairs + hp, 0, 0))]
                     + [page_spec(p) for p in range(pps)],
            out_specs=[pl.BlockSpec((1, blk, LANES), qmap),
                       pl.BlockSpec((1, rows, keys), lambda b, hp, i, pt: (b * n_pairs + hp, 0, i)),
                       pl.BlockSpec((1, 1, rows, LANES), lambda b, hp, i, pt: (b * n_pairs + hp, i, 0, 0))],
            scratch_shapes=[pltpu.VMEM((2, nb8, MOBA_HEAD_DIM), F32)]),
        compiler_params=_cparams(("parallel", "parallel", "arbitrary")),
        name="moba_prompt",
    )(page_table, qm, kt, vt, q_new, *([k_pages] * pps))


def _gla_out(o, gg, ng):
    ms = jnp.mean(o * o, axis=1, keepdims=True)
    return o * lax.rsqrt(ms + RMS_EPS) * ng * _silu(gg)


def _gla_prompt_kernel(q_ref, k_ref, la_ref, v_ref, gg_ref, ng_ref, o_ref, s_ref, st_scr):
    c = GLA_CHUNK
    n_chunks = q_ref.shape[1] // c
    lane = lax.broadcasted_iota(jnp.int32, (1, LANES), 1)
    row = lax.broadcasted_iota(jnp.int32, (c, 1), 0)
    col = lax.broadcasted_iota(jnp.int32, (1, c), 1)
    tril_bf = jnp.where(col <= row, 1.0, 0.0).astype(BF16)
    ng = ng_ref[...]
    st_scr[...] = jnp.zeros_like(st_scr)

    halves = []
    half = c // 2
    while half >= GLA_LEAF:
        halves.append(half)
        half //= 2
    def same_block(size):
        sh = size.bit_length() - 1
        return lax.shift_right_logical(row, sh) == lax.shift_right_logical(col, sh)

    pair_masks = [same_block(2 * hf) & ((row & hf) != 0) & ((col & hf) == 0) for hf in halves]
    leaf_mask = same_block(GLA_LEAF) & (col <= row)

    def rows_of(b, size, pick):
        parts = []
        for n in range(c // size):
            r = pick(n)
            src = b[r:r + 1, :] if r >= 0 else jnp.zeros((1, LANES), F32)
            parts.append(jnp.broadcast_to(src, (size, LANES)))
        return jnp.concatenate(parts, axis=0)

    def body(ci, carry):
        r0 = pl.multiple_of(ci * c, c)
        for hp in range(GLA_HEADS // 2):
            sl = slice(hp * LANES, (hp + 1) * LANES)
            q = q_ref[0, pl.ds(r0, c), sl]
            k = k_ref[0, pl.ds(r0, c), sl]
            b = _dot_exact_lhs(tril_bf, la_ref[0, pl.ds(r0, c), sl])
            b_end = b[c - 1:c, :]
            q_dec = q * jnp.exp(b)
            k_dec = k * jnp.exp(b_end - b)
            e_end = jnp.exp(b_end)
            level_qk = []
            for hf in halves:
                u = jnp.exp(-jnp.abs(b - rows_of(b, 2 * hf, lambda n: n * 2 * hf + hf - 1)))
                level_qk.append((q * u, k * u))
            b_leaf = b - rows_of(b, GLA_LEAF, lambda n: n * GLA_LEAF - 1)
            q_leaf, k_leaf = q * jnp.exp(b_leaf), k * jnp.exp(-b_leaf)
            for hh in range(2):
                h = 2 * hp + hh
                hm = (lane >= GLA_DK * hh) & (lane < GLA_DK * (hh + 1))
                qb = jnp.where(hm, q_dec, 0.0)
                vh = v_ref[0, pl.ds(r0, c), h * GLA_DV:(h + 1) * GLA_DV]
                att = jnp.where(leaf_mask, _dot_nt(jnp.where(hm, q_leaf, 0.0), k_leaf), 0.0)
                for mask, (ql, kl) in zip(pair_masks, level_qk):
                    att = jnp.where(mask, _dot_nt(jnp.where(hm, ql, 0.0), kl), att)
                st = st_scr[h]
                o = _dot(att, vh) + _dot_nt(qb, st)
                st_scr[h] = st * e_end + _dot(vh.T, jnp.where(hm, k_dec, 0.0))
                gg = gg_ref[0, pl.ds(r0, c), h * GLA_DV:(h + 1) * GLA_DV]
                o_ref[0, pl.ds(r0, c), h * GLA_DV:(h + 1) * GLA_DV] = _gla_out(o, gg, ng)
        return carry

    lax.fori_loop(0, n_chunks, body, 0, unroll=2)
    for h in range(GLA_HEADS):
        hh = h % 2
        s_ref[0, 0, h] = st_scr[h].T[hh * GLA_DK:(hh + 1) * GLA_DK, :]


def _gla_prompt(qg, kg, la, vg, gg, norm_g):
    nb, s, _ = qg.shape
    assert s % GLA_CHUNK == 0
    map3 = lambda b: (b, 0, 0)
    return pl.pallas_call(
        _gla_prompt_kernel,
        out_shape=[jax.ShapeDtypeStruct((nb, s, D_GLA), F32),
                   jax.ShapeDtypeStruct((1, nb, GLA_HEADS, GLA_DK, GLA_DV), F32)],
        grid=(nb,),
        in_specs=[pl.BlockSpec((1, s, D_GLA_K), map3),
                  pl.BlockSpec((1, s, D_GLA_K), map3),
                  pl.BlockSpec((1, s, D_GLA_K), map3),
                  pl.BlockSpec((1, s, D_GLA), map3),
                  pl.BlockSpec((1, s, D_GLA), map3),
                  pl.BlockSpec((1, GLA_DV), lambda b: (0, 0))],
        out_specs=[pl.BlockSpec((1, s, D_GLA), map3),
                   pl.BlockSpec((1, 1, GLA_HEADS, GLA_DK, GLA_DV), lambda b: (0, b, 0, 0, 0))],
        scratch_shapes=[pltpu.VMEM((GLA_HEADS, GLA_DV, LANES), F32)],
        compiler_params=_cparams(("parallel",)),
        name="gla_prompt",
    )(qg, kg, la, vg, gg, norm_g.reshape(1, GLA_DV))


def _gla_sample_kernel(q_ref, k_ref, la_ref, v_ref, gg_ref, s0_ref, ng_ref, o_ref, s_ref):
    rows_per_step, t = q_ref.shape[0], q_ref.shape[1]
    lane = lax.broadcasted_iota(jnp.int32, (1, LANES), 1)
    row = lax.broadcasted_iota(jnp.int32, (t, 1), 0)
    ng = ng_ref[...]
    eye = (lax.broadcasted_iota(jnp.int32, (GLA_DK, 1), 0)
           == lax.broadcasted_iota(jnp.int32, (1, GLA_DK), 1))
    zeros_half = jnp.zeros((GLA_DK, GLA_DV), F32)
    for bi in range(rows_per_step):
        for hp in range(GLA_HEADS // 2):
            sl = slice(hp * LANES, (hp + 1) * LANES)
            q = q_ref[bi, :, sl]
            k = k_ref[bi, :, sl]
            b = la_ref[bi, :, sl]
            sh = 1
            while sh < t:
                b = b + jnp.where(row >= sh, pltpu.roll(b, sh, 0), 0.0)
                sh *= 2
            b_end = b[t - 1:t, :]
            e_end = jnp.exp(b_end)
            q_dec = q * jnp.exp(b)
            k_inv = _pad_rows(k * jnp.exp(-b), LANES)
            k_dec = k * jnp.exp(b_end - b)
            for hh in range(2):
                h = 2 * hp + hh
                hm = (lane >= GLA_DK * hh) & (lane < GLA_DK * (hh + 1))
                qb = jnp.where(hm, q_dec, 0.0)
                vh = _pad_rows(v_ref[bi, :, h * GLA_DV:(h + 1) * GLA_DV], LANES)
                s0 = s0_ref[bi, h]
                s0_pad = jnp.concatenate([s0, zeros_half] if hh == 0 else [zeros_half, s0], axis=0)
                att = jnp.where(lane <= row, _dot_nt(qb, k_inv), 0.0)
                o = _dot(att, vh) + _dot(qb, s0_pad)
                gg = gg_ref[bi, :, h * GLA_DV:(h + 1) * GLA_DV]
                o_ref[bi, :, h * GLA_DV:(h + 1) * GLA_DV] = _gla_out(o, gg, ng)
                kd = _pad_rows(jnp.where(hm, k_dec, 0.0), LANES)
                upd = _dot(kd.T, vh)[hh * GLA_DK:(hh + 1) * GLA_DK, :]
                e_h = e_end[:, hh * GLA_DK:(hh + 1) * GLA_DK]
                diag = jnp.where(eye, jnp.broadcast_to(e_h, (GLA_DK, GLA_DK)), 0.0)
                s_ref[0, bi, h] = _dot3(diag, s0) + upd


def _gla_sample(qg, kg, la, vg, gg, s0, norm_g, rows_per_step=4):
    nb, t, _ = qg.shape
    r = rows_per_step
    assert nb % r == 0
    map3 = lambda b: (b, 0, 0)
    return pl.pallas_call(
        _gla_sample_kernel,
        out_shape=[jax.ShapeDtypeStruct((nb, t, D_GLA), F32),
                   jax.ShapeDtypeStruct((1, nb, GLA_HEADS, GLA_DK, GLA_DV), F32)],
        grid=(nb // r,),
        in_specs=[pl.BlockSpec((r, t, D_GLA_K), map3),
                  pl.BlockSpec((r, t, D_GLA_K), map3),
                  pl.BlockSpec((r, t, D_GLA_K), map3),
                  pl.BlockSpec((r, t, D_GLA), map3),
                  pl.BlockSpec((r, t, D_GLA), map3),
                  pl.BlockSpec((r, GLA_HEADS, GLA_DK, GLA_DV), lambda b: (b, 0, 0, 0)),
                  pl.BlockSpec((1, GLA_DV), lambda b: (0, 0))],
        out_specs=[pl.BlockSpec((r, t, D_GLA), map3),
                   pl.BlockSpec((1, r, GLA_HEADS, GLA_DK, GLA_DV), lambda b: (0, b, 0, 0, 0))],
        compiler_params=_cparams(("parallel",)),
        name="gla_sample",
    )(qg, kg, la, vg, gg, s0, norm_g.reshape(1, GLA_DV))


def _block_diag_queries(q):
    lane = lax.broadcasted_iota(jnp.int32, (1, D_MOBA), 1)
    parts = [jnp.where((lane >= MOBA_HEAD_DIM * h) & (lane < MOBA_HEAD_DIM * (h + 1)), q, 0.0)
             for h in range(MOBA_HEADS)]
    return jnp.concatenate(parts, axis=0)


def _moba_attend_kernel(pt_ref, q_ref, g_ref, sc_ref, kn_ref, vn_ref, *refs, n_full):
    del pt_ref
    n = PAGES_PER_STEP
    pages, o_ref = refs[:n], refs[n]
    sel_scr, m_scr, l_scr, acc_scr = refs[n + 1:]
    s = pl.program_id(1)
    t = q_ref.shape[1]
    rows = MOBA_HEADS * t
    keys = n * PAGE_SIZE
    bps = keys // MOBA_BLOCK
    lane = lax.broadcasted_iota(jnp.int32, (1, LANES), 1)
    lane_f = lane.astype(F32)

    @pl.when(s == 0)
    def _():
        gbps = n_full // g_ref.shape[1]
        g = jnp.zeros((rows, LANES), F32)
        for st in range(g_ref.shape[1]):
            g = g + pltpu.roll(g_ref[0, st], st * gbps, 1)
        g = jnp.where(lane < n_full, g, -jnp.inf)
        sel = jnp.zeros((rows, LANES), F32)
        for _ in range(MOBA_TOPK):
            mx = jnp.max(g, axis=1, keepdims=True)
            first = jnp.min(jnp.where(g == mx, lane_f, float(LANES)), axis=1, keepdims=True)
            pick = lane_f == first
            sel = jnp.where(pick, 1.0, sel)
            g = jnp.where(pick, -jnp.inf, g)
        sel_scr[...] = sel
        m_scr[...] = jnp.full(m_scr.shape, NEG_BIG, F32)
        l_scr[...] = jnp.zeros_like(l_scr)
        acc_scr[...] = jnp.zeros_like(acc_scr)

    col = lax.broadcasted_iota(jnp.int32, (1, keys), 1)
    blk_of_col = lax.shift_right_logical(col, MOBA_BLOCK.bit_length() - 1) + s * bps
    expand = jnp.where(lax.broadcasted_iota(jnp.int32, (LANES, 1), 0) == blk_of_col, 1.0, 0.0)
    pmask = _dot(sel_scr[...], expand) > 0.5

    def pv(p):
        p = p.astype(BF16)
        out = jnp.zeros(acc_scr.shape, F32)
        for pg in range(n):
            out += _dot_nt(p[:, pg * PAGE_SIZE:(pg + 1) * PAGE_SIZE], pages[pg][0])
        return out

    _softmax_step(sc_ref[0], pmask, pv, m_scr, l_scr, acc_scr)

    @pl.when(s == pl.num_programs(1) - 1)
    def _():
        qbd = _block_diag_queries(q_ref[0]) * (MOBA_HEAD_DIM ** -0.5)
        row_t = lax.broadcasted_iota(jnp.int32, (rows, 1), 0) & (t - 1)
        s_own = _dot_nt(qbd, _pad_rows(kn_ref[0], LANES))
        v_own = _pad_rows(vn_ref[0], LANES)
        _softmax_step(s_own, lane <= row_t, lambda p: _dot(p, v_own), m_scr, l_scr, acc_scr)
        out = acc_scr[...] / l_scr[...]
        lane_w = lax.broadcasted_iota(jnp.int32, (1, D_MOBA), 1)
        o = jnp.zeros((t, D_MOBA), F32)
        for h in range(MOBA_HEADS):
            hm = (lane_w >= MOBA_HEAD_DIM * h) & (lane_w < MOBA_HEAD_DIM * (h + 1))
            o = o + jnp.where(hm, out[h * t:(h + 1) * t, :], 0.0)
        o_ref[0] = o


def _moba_sample(qm, km, vm, scores, gates, v_pages, page_table):
    nb, t, _ = qm.shape
    n_pages = page_table.shape[1]
    past = n_pages * PAGE_SIZE
    n_full = past // MOBA_BLOCK
    assert past % MOBA_BLOCK == 0 and MOBA_TOPK <= n_full <= LANES and t & (t - 1) == 0
    n = PAGES_PER_STEP
    assert n_pages % n == 0 and (n * PAGE_SIZE) % MOBA_BLOCK == 0 and n_full % gates.shape[1] == 0
    steps = n_pages // n
    keys = n * PAGE_SIZE
    rows = MOBA_HEADS * t

    def page_spec(p):
        return pl.BlockSpec((1, D_MOBA, PAGE_SIZE), lambda b, s, pt, p=p: (pt[b, s * n + p], 0, 0))

    qspec = pl.BlockSpec((1, t, D_MOBA), lambda b, s, pt: (b, 0, 0))
    return pl.pallas_call(
        functools.partial(_moba_attend_kernel, n_full=n_full),
        out_shape=jax.ShapeDtypeStruct((nb, t, D_MOBA), F32),
        grid_spec=pltpu.PrefetchScalarGridSpec(
            num_scalar_prefetch=1, grid=(nb, steps),
            in_specs=[qspec,
                      pl.BlockSpec((1,) + gates.shape[1:], lambda b, s, pt: (b, 0, 0, 0)),
                      pl.BlockSpec((1, rows, keys), lambda b, s, pt: (b, 0, s)),
                      qspec, qspec] + [page_spec(p) for p in range(n)],
            out_specs=qspec,
            scratch_shapes=[pltpu.VMEM((rows, LANES), F32), pltpu.VMEM((rows, 1), F32),
                            pltpu.VMEM((rows, 1), F32), pltpu.VMEM((rows, D_MOBA), F32)]),
        compiler_params=_cparams(("parallel", "arbitrary")),
        name="moba_sample_attend",
    )(page_table, qm, gates, scores, km, vm, *([v_pages] * n))


def kernel(x_prompt, x_sample, cache_k, cache_v, state_gla, page_table, c_prompt, c_sample, w_ada, b_ada, ln_g, ln_b, w_ffn1_in, w_ffn1_out, w_mix_in, w_gk_up, b_gk, gla_norm_g, w_mix_out, w_ffn2_in, w_ffn2_out):
    depth = w_ada.shape[0]
    assert depth == 1, "one decoder layer"
    d = x_prompt.shape[-1]
    alpha = (2.0 * depth) ** 0.25
    nb_p, s_p, _ = x_prompt.shape
    nb_s, t_s, _ = x_sample.shape
    past_len = page_table.shape[1] * PAGE_SIZE

    w1_in, w1_out = w_ffn1_in[0].astype(BF16), w_ffn1_out[0].astype(BF16)
    w2_in, w2_out = w_ffn2_in[0].astype(BF16), w_ffn2_out[0].astype(BF16)
    w_mix = w_mix_in[0]
    w_main = w_mix[:, :D_MIX_MAIN].astype(BF16)
    w_rg = jnp.pad(w_mix[:, D_MIX_MAIN:], ((0, 0), (0, LANES - GLA_GATE_RANK))).astype(BF16)
    w_gk = jnp.pad(w_gk_up[0], ((0, LANES - GLA_GATE_RANK), (0, 0)))
    w_mo = w_mix_out[0].astype(BF16)

    c_all = jnp.concatenate([c_sample, c_prompt], axis=0)
    m = _ada_modulation(c_all, w_ada[0], b_ada[0])
    m4 = m.reshape(3 * N_SUBLAYERS, c_all.shape[0], 1, d)

    def first_half(x, m_row0, pos_base, kv_transposed):
        x = _ffn(x, m4, m_row0, 0, w1_in, w1_out, ln_g[0, 0], ln_b[0, 0], alpha)
        return x, _mixer_in(x, m4, m_row0, w_main, w_rg, w_gk, b_gk[0], pos_base, kv_transposed)

    def second_half(x, m_row0, o_moba, o_gla):
        return _ffn(x, m4, m_row0, 2, w2_in, w2_out, ln_g[0, 2], ln_b[0, 2], alpha,
                    mixer_out=(o_moba, o_gla, w_mo, ln_g[0, 1], ln_b[0, 1]))

    k_pages = jnp.transpose(cache_k[0], (0, 2, 3, 1)).reshape(cache_k.shape[1], D_MOBA, PAGE_SIZE)
    v_pages = jnp.transpose(cache_v[0], (0, 2, 3, 1)).reshape(cache_v.shape[1], D_MOBA, PAGE_SIZE)

    xs, (qm_s, ks, vs, qg_s, kg_s, vg_s, gg_s, la_s) = first_half(x_sample, 0, past_len, False)
    xp, (qm_p, kp_t, vp_t, qg_p, kg_p, vg_p, gg_p, la_p) = first_half(x_prompt, nb_s, 0, True)
    o_moba_p, scores, gates = _moba_prompt(qm_p, kp_t, vp_t, qm_s, k_pages, page_table)
    o_gla_p, sp = _gla_prompt(qg_p, kg_p, la_p, vg_p, gg_p, gla_norm_g[0])
    yp = second_half(xp, nb_s, o_moba_p, o_gla_p)
    o_moba_s = _moba_sample(qm_s, ks, vs, scores, gates, v_pages, page_table)
    o_gla_s, ss = _gla_sample(qg_s, kg_s, la_s, vg_s, gg_s, state_gla[0], gla_norm_g[0])
    ys = second_half(xs, 0, o_moba_s, o_gla_s)

    def rows_major(a_t):
        a = a_t.reshape(nb_p, MOBA_HEADS, MOBA_HEAD_DIM, s_p)
        return jnp.transpose(a, (0, 3, 1, 2))[None]

    def heads(a):
        return a.reshape(1, nb_s, t_s, MOBA_HEADS, MOBA_HEAD_DIM)

    return (yp, ys, rows_major(kp_t), rows_major(vp_t), sp, heads(ks), heads(vs), ss)
```

```python
import functools

import jax
import jax.numpy as jnp
from jax import lax
from jax.experimental import pallas as pl
from jax.experimental.pallas import tpu as pltpu

F32 = jnp.float32
BF16 = jnp.bfloat16

PAGE_SIZE = 128
MOBA_HEADS = 8
MOBA_HEAD_DIM = 64
D_MOBA = MOBA_HEADS * MOBA_HEAD_DIM
MOBA_BLOCK = 256
MOBA_TOPK = 3
ROPE_THETA = 500000.0
ROPE_DIMS = MOBA_HEAD_DIM // 4
ROPE_HALF = ROPE_DIMS // 2
GLA_HEADS = 4
GLA_DK = 64
GLA_DV = 128
D_GLA_K = GLA_HEADS * GLA_DK
D_GLA = GLA_HEADS * GLA_DV
GLA_GATE_RANK = 16
GLA_GATE_NORM = 16.0
D_MIX_MAIN = 3 * D_MOBA + 2 * D_GLA_K + 2 * D_GLA
N_SUBLAYERS = 3
LN_EPS = 1e-5
RMS_EPS = 1e-6

LANES = 128
NEG_BIG = -1e30
VMEM_LIMIT = 56 * 1024 * 1024

FFN_ROWS = 512
FFN_CHUNKS = 1
GLA_CHUNK = 128
GLA_LEAF = 32
PAGES_PER_STEP = 64


def _cparams(sem):
    return pltpu.CompilerParams(dimension_semantics=sem, vmem_limit_bytes=VMEM_LIMIT)


def _dot(a, b):
    return jnp.dot(a.astype(BF16), b.astype(BF16), preferred_element_type=F32)


def _dot_nt(a, b):
    return lax.dot_general(a.astype(BF16), b.astype(BF16), (((1,), (1,)), ((), ())),
                           preferred_element_type=F32)


def _split2(x):
    hi = x.astype(BF16)
    lo = (x - hi.astype(F32)).astype(BF16)
    return hi, lo


def _dot3(a, b, nt=False):
    d = _dot_nt if nt else _dot
    ah, al = _split2(a)
    bh, bl = _split2(b)
    return d(ah, bh) + (d(ah, bl) + d(al, bh))


def _dot_exact_lhs(lhs_bf16, x):
    hi = x.astype(BF16)
    r1 = x - hi.astype(F32)
    mid = r1.astype(BF16)
    lo = (r1 - mid.astype(F32)).astype(BF16)
    f = functools.partial(jnp.dot, lhs_bf16, preferred_element_type=F32)
    return f(hi) + (f(mid) + f(lo))


def _silu(x):
    return x * jax.nn.sigmoid(x)


def _layer_norm(y, g, b):
    mu = jnp.mean(y, axis=-1, keepdims=True)
    yc = y - mu
    var = jnp.mean(yc * yc, axis=-1, keepdims=True)
    return yc * lax.rsqrt(var + LN_EPS) * g + b


def _pad_rows(x, rows):
    return jnp.concatenate([x, jnp.zeros((rows - x.shape[0], x.shape[1]), x.dtype)], axis=0)


def _ada_kernel(c_ref, w_ref, b_ref, o_ref):
    o_ref[0] = _dot3(_silu(c_ref[...]), w_ref[...]) + b_ref[...]


def _ada_modulation(c_all, w_ada, b_ada):
    nb, d = c_all.shape
    n_out = w_ada.shape[1] // d
    return pl.pallas_call(
        _ada_kernel,
        out_shape=jax.ShapeDtypeStruct((n_out, nb, d), F32),
        grid=(n_out,),
        in_specs=[pl.BlockSpec((nb, d), lambda n: (0, 0)),
                  pl.BlockSpec((d, d), lambda n: (0, n)),
                  pl.BlockSpec((1, d), lambda n: (0, n))],
        out_specs=pl.BlockSpec((1, nb, d), lambda n: (n, 0, 0)),
        compiler_params=_cparams(("arbitrary",)),
        name="ada_modulation",
    )(c_all, w_ada, b_ada.reshape(1, -1))


def _ffn_kernel(*refs, alpha, fused_mixer_out, n_chunks):
    if fused_mixer_out:
        x_ref, om_ref, og_ref, mm_ref, wmo_ref, gm_ref, bm_ref = refs[:7]
        m_ref, wa_ref, wu_ref, wo_ref, g_ref, b_ref, o_ref, h_scr, acc_scr, res_scr = refs[7:]
    else:
        x_ref, m_ref, wa_ref, wu_ref, wo_ref, g_ref, b_ref, o_ref, h_scr, acc_scr = refs
        res_scr = x_ref
    j = pl.program_id(1)
    bb, ts, d = x_ref.shape
    tm = bb * ts

    def modulated_input():
        x = x_ref[...]
        if fused_mixer_out:
            mix = (_dot(om_ref[...].reshape(tm, D_MOBA), wmo_ref[0:D_MOBA, :])
                   + _dot(og_ref[...].reshape(tm, D_GLA), wmo_ref[D_MOBA:D_MOBA + D_GLA, :]))
            x = _layer_norm(alpha * x + (1.0 + mm_ref[2]) * mix.reshape(bb, ts, d), gm_ref[...], bm_ref[...])
            res_scr[...] = x
        h = (x * (1.0 + m_ref[1]) + m_ref[0]).reshape(tm, d).astype(BF16)
        h_scr[...] = h
        return h

    def chunk(h):
        a = jnp.dot(h, wa_ref[...], preferred_element_type=F32)
        u = jnp.dot(h, wu_ref[...], preferred_element_type=F32)
        t = (_silu(a) * u).astype(BF16)
        return jnp.dot(t, wo_ref[...], preferred_element_type=F32)

    def post_norm(acc):
        y = alpha * res_scr[...] + (0.5 * (1.0 + m_ref[2])) * acc.reshape(bb, ts, d)
        o_ref[...] = _layer_norm(y, g_ref[...], b_ref[...])

    @pl.when(j == 0)
    def _():
        acc = chunk(modulated_input())
        if n_chunks == 1:
            post_norm(acc)
        else:
            acc_scr[...] = acc

    if n_chunks > 2:
        @pl.when((j > 0) & (j < n_chunks - 1))
        def _():
            acc_scr[...] += chunk(h_scr[...])

    if n_chunks > 1:
        @pl.when(j == n_chunks - 1)
        def _():
            post_norm(acc_scr[...] + chunk(h_scr[...]))


def _row_tiling(x):
    nb, s, _ = x.shape
    if s >= FFN_ROWS:
        assert s % FFN_ROWS == 0
        return 1, FFN_ROWS
    assert s % 8 == 0 and FFN_ROWS % s == 0
    bb = min(nb, FFN_ROWS // s)
    assert nb % bb == 0
    return bb, s


def _ffn(x, m4, m_row0, sub, w_in, w_out, ln_g, ln_b, alpha, mixer_out=None, n_chunks=FFN_CHUNKS):
    nb, s, d = x.shape
    bb, ts = _row_tiling(x)
    tpb = s // ts
    d_ff = w_out.shape[0]
    ck = d_ff // n_chunks
    assert ck * n_chunks == d_ff and ck % LANES == 0 and m_row0 % bb == 0
    grid = ((nb // bb) * tpb, n_chunks)
    xmap = lambda i, j: (i // tpb, i % tpb, 0)
    const2 = lambda i, j: (0, 0)

    def mspec(k):
        return pl.BlockSpec((3, bb, 1, d), lambda i, j: (k, m_row0 // bb + i // tpb, 0, 0))

    vec = pl.BlockSpec((1, d), const2)
    in_specs = [pl.BlockSpec((bb, ts, d), xmap)]
    args = [x]
    scratch = [pltpu.VMEM((bb * ts, d), BF16), pltpu.VMEM((bb * ts, d), F32)]
    if mixer_out is not None:
        o_moba, o_gla, w_mo, g_mo, b_mo = mixer_out
        in_specs += [pl.BlockSpec((bb, ts, D_MOBA), xmap), pl.BlockSpec((bb, ts, D_GLA), xmap),
                     mspec(1), pl.BlockSpec(w_mo.shape, const2), vec, vec]
        args += [o_moba, o_gla, m4, w_mo, g_mo.reshape(1, d), b_mo.reshape(1, d)]
        scratch.append(pltpu.VMEM((bb, ts, d), F32))
    in_specs += [mspec(sub),
                 pl.BlockSpec((d, ck), lambda i, j: (0, j)),
                 pl.BlockSpec((d, ck), lambda i, j: (0, j + n_chunks)),
                 pl.BlockSpec((ck, d), lambda i, j: (j, 0)),
                 vec, vec]
    args += [m4, w_in, w_in, w_out, ln_g.reshape(1, d), ln_b.reshape(1, d)]
    return pl.pallas_call(
        functools.partial(_ffn_kernel, alpha=alpha, fused_mixer_out=mixer_out is not None,
                          n_chunks=n_chunks),
        out_shape=jax.ShapeDtypeStruct(x.shape, F32),
        grid=grid,
        in_specs=in_specs,
        out_specs=pl.BlockSpec((bb, ts, d), xmap),
        scratch_shapes=scratch,
        compiler_params=_cparams(("parallel", "arbitrary")),
        name="ffn_postnorm",
    )(*args)


def _rope_table_kernel(o_ref, *, pos_base):
    tt = o_ref.shape[1]
    lane = lax.broadcasted_iota(jnp.int32, (1, LANES), 1)
    fi = (lane & (ROPE_HALF - 1)).astype(F32)
    inv = jnp.power(jnp.full((1, LANES), ROPE_THETA, F32), -fi / ROPE_HALF)
    row = lax.broadcasted_iota(jnp.int32, (tt, 1), 0)
    pos = (pos_base + pl.program_id(0) * tt + row).astype(F32)
    ang = pos * inv
    cos = jnp.cos(ang)
    sin = jnp.sin(ang)
    l64 = lane & (MOBA_HEAD_DIM - 1)
    o_ref[0] = jnp.where(l64 < ROPE_DIMS, cos, 1.0)
    o_ref[1] = jnp.where(l64 < ROPE_HALF, -sin, 0.0)
    o_ref[2] = jnp.where((l64 >= ROPE_HALF) & (l64 < ROPE_DIMS), sin, 0.0)


def _rope_tables(n_pos, tile, pos_base):
    return pl.pallas_call(
        functools.partial(_rope_table_kernel, pos_base=pos_base),
        out_shape=jax.ShapeDtypeStruct((3, n_pos, LANES), F32),
        grid=(n_pos // tile,),
        in_specs=[],
        out_specs=pl.BlockSpec((3, tile, LANES), lambda i: (0, i, 0)),
        compiler_params=_cparams(("arbitrary",)),
        name="rope_tables",
    )()


def _mixin_kernel(x_ref, m_ref, tab_ref, w_ref, wrg_ref, wgk_ref, bgk_ref,
                  qm_ref, km_ref, vm_ref, qg_ref, kg_ref, vg_ref, gg_ref, la_ref,
                  *, kv_transposed):
    bb, ts, d = x_ref.shape
    tm = bb * ts
    h = (x_ref[...] * (1.0 + m_ref[1]) + m_ref[0]).reshape(tm, d).astype(BF16)
    p = jnp.dot(h, w_ref[...], preferred_element_type=F32)
    c_tab, s_lo, s_hi = tab_ref[0], tab_ref[1], tab_ref[2]

    def rope(x):
        slabs = []
        for s in range(x.shape[1] // LANES):
            xs = x[:, s * LANES:(s + 1) * LANES]
            r = (xs.reshape(bb, ts, LANES) * c_tab
                 + pltpu.roll(xs, LANES - ROPE_HALF, 1).reshape(bb, ts, LANES) * s_lo
                 + pltpu.roll(xs, ROPE_HALF, 1).reshape(bb, ts, LANES) * s_hi)
            slabs.append(r.reshape(tm, LANES))
        return jnp.concatenate(slabs, axis=1)

    def store(ref, x):
        ref[...] = x.reshape(ref.shape)

    def store_kv(ref, x):
        if kv_transposed:
            ref[0] = x.T
        else:
            store(ref, x)

    o = 0
    store(qm_ref, rope(p[:, o:o + D_MOBA])); o += D_MOBA
    store_kv(km_ref, rope(p[:, o:o + D_MOBA])); o += D_MOBA
    store_kv(vm_ref, p[:, o:o + D_MOBA]); o += D_MOBA
    store(qg_ref, p[:, o:o + D_GLA_K] * (GLA_DK ** -0.5)); o += D_GLA_K
    store(kg_ref, p[:, o:o + D_GLA_K]); o += D_GLA_K
    store(vg_ref, p[:, o:o + D_GLA]); o += D_GLA
    store(gg_ref, p[:, o:o + D_GLA]); o += D_GLA

    rg = jnp.dot(h, wrg_ref[...], preferred_element_type=F32)
    z = _dot3(rg, wgk_ref[...]) + bgk_ref[...]
    log_sig = jnp.minimum(z, 0.0) - jnp.log1p(jnp.exp(-jnp.abs(z)))
    store(la_ref, log_sig / GLA_GATE_NORM)


def _mixer_in(x, m4, m_row0, w_main, w_rg, w_gk, b_gk, pos_base, kv_transposed):
    nb, s, d = x.shape
    bb, ts = _row_tiling(x)
    tpb = s // ts
    assert m_row0 % bb == 0 and (bb == 1 or not kv_transposed)
    grid = ((nb // bb) * tpb,)
    xmap = lambda i: (i // tpb, i % tpb, 0)
    tmap = lambda i: (i // tpb, 0, i % tpb)
    mmap = lambda i: (1, m_row0 // bb + i // tpb, 0, 0)
    const2 = lambda i: (0, 0)
    widths = (D_MOBA, D_MOBA, D_MOBA, D_GLA_K, D_GLA_K, D_GLA, D_GLA, D_GLA_K)
    out_shape = [jax.ShapeDtypeStruct((nb, s, w), F32) for w in widths]
    out_specs = [pl.BlockSpec((bb, ts, w), xmap) for w in widths]
    if kv_transposed:
        for n in (1, 2):
            out_shape[n] = jax.ShapeDtypeStruct((nb, D_MOBA, s), F32)
            out_specs[n] = pl.BlockSpec((1, D_MOBA, ts), tmap)
    tables = _rope_tables(s, ts, pos_base)
    return pl.pallas_call(
        functools.partial(_mixin_kernel, kv_transposed=kv_transposed),
        out_shape=out_shape,
        grid=grid,
        in_specs=[pl.BlockSpec((bb, ts, d), xmap),
                  pl.BlockSpec((3, bb, 1, d), mmap),
                  pl.BlockSpec((3, ts, LANES), lambda i: (0, i % tpb, 0)),
                  pl.BlockSpec(w_main.shape, const2),
                  pl.BlockSpec(w_rg.shape, const2),
                  pl.BlockSpec(w_gk.shape, const2),
                  pl.BlockSpec((1, D_GLA_K), const2)],
        out_specs=out_specs,
        compiler_params=_cparams(("parallel",)),
        name="mixer_in",
    )(x, m4, tables, w_main, w_rg, w_gk, b_gk.reshape(1, -1))


def _softmax_step(s, pmask, pv, m_scr, l_scr, acc_scr):
    m_old = m_scr[...]
    m_new = jnp.maximum(m_old, jnp.max(jnp.where(pmask, s, NEG_BIG), axis=1, keepdims=True))
    p = jnp.where(pmask, jnp.exp(s - m_new), 0.0)
    alpha = jnp.exp(m_old - m_new)
    l_scr[...] = alpha * l_scr[...] + jnp.sum(p, axis=1, keepdims=True)
    acc_scr[...] = alpha * acc_scr[...] + pv(p)
    m_scr[...] = m_new


def _moba_prompt_kernel(pt_ref, q_ref, kt_ref, vt_ref, qs_ref, *refs, pages_per_step):
    del pt_ref
    pages = refs[:pages_per_step]
    o_ref, sc_ref, g_ref, kmean_scr = refs[pages_per_step:]
    i = pl.program_id(2)
    blk = q_ref.shape[1]
    s_len = kt_ref.shape[2]
    n_blocks = s_len // blk
    nb8 = kmean_scr.shape[1]
    hd = MOBA_HEAD_DIM
    blk_shift = blk.bit_length() - 1
    blk_row = lax.broadcasted_iota(jnp.int32, (nb8, 1), 0)

    @pl.when(i == 0)
    def _():
        blk_of_key = lax.shift_right_logical(lax.broadcasted_iota(jnp.int32, (1, s_len), 1), blk_shift)
        pool = jnp.where(blk_row == blk_of_key, 1.0 / blk, 0.0).astype(BF16)
        for h in range(2):
            kth = kt_ref[0, h * hd:(h + 1) * hd, :]
            hi = kth.astype(BF16)
            r1 = kth - hi.astype(F32)
            mid = r1.astype(BF16)
            lo = (r1 - mid.astype(F32)).astype(BF16)
            kmean_scr[h] = _dot_nt(pool, hi) + (_dot_nt(pool, mid) + _dot_nt(pool, lo))

    row = lax.broadcasted_iota(jnp.int32, (blk, 1), 0)
    col = lax.broadcasted_iota(jnp.int32, (1, blk), 1)
    causal = col <= row

    def picked_blocks(qh, h, own):
        gt = jnp.where(blk_row < own, _dot3(kmean_scr[h], qh, nt=True), -jnp.inf)
        rank = jnp.zeros(gt.shape, F32)
        for m in range(own):
            gm = gt[m:m + 1, :]
            beats = (gm > gt) | ((gm == gt) & (m < blk_row))
            rank += jnp.where(beats, 1.0, 0.0)
        sel_t = jnp.where((blk_row < own) & (rank < MOBA_TOPK), 1.0, 0.0)
        return _pad_rows(sel_t, LANES).T

    def attend(own):
        q = q_ref[0]
        n = (own + 1) * blk
        outs = []
        for h in range(2):
            qh = q[:, h * hd:(h + 1) * hd]
            s = _dot((qh * (hd ** -0.5)).astype(BF16), kt_ref[0, h * hd:(h + 1) * hd, 0:n])
            if own <= MOBA_TOPK:
                parts = [s[:, 0:own * blk]] if own else []
            else:
                sel = picked_blocks(qh, h, own)
                parts = []
                for j in range(own):
                    picked = jnp.broadcast_to(sel[:, j:j + 1], (blk, blk)) > 0.5
                    parts.append(jnp.where(picked, s[:, j * blk:(j + 1) * blk], -jnp.inf))
            parts.append(jnp.where(causal, s[:, own * blk:n], -jnp.inf))
            sm = jnp.concatenate(parts, axis=1)
            p = jnp.exp(sm - jnp.max(sm, axis=1, keepdims=True))
            l = jnp.sum(p, axis=1, keepdims=True)
            outs.append(_dot_nt(p, vt_ref[0, h * hd:(h + 1) * hd, 0:n]) / l)
        return jnp.concatenate(outs, axis=1)

    def score_pages():
        qbd = (_block_diag_queries(qs_ref[0]) * (hd ** -0.5)).astype(BF16)
        lane = lax.broadcasted_iota(jnp.int32, (1, LANES), 1)
        ppb = MOBA_BLOCK // PAGE_SIZE
        g = jnp.zeros((qbd.shape[0], LANES), F32)
        for p in range(pages_per_step):
            s = _dot(qbd, pages[p][0])
            sc_ref[0, :, p * PAGE_SIZE:(p + 1) * PAGE_SIZE] = s.astype(sc_ref.dtype)
            psum = jnp.sum(s, axis=1, keepdims=True)
            bs = psum if p % ppb == 0 else bs + psum
            if p % ppb == ppb - 1:
                g = jnp.where(lane == p // ppb, bs, g)
        g_ref[0, 0] = g

    for own in range(n_blocks):
        @pl.when(i == own)
        def _(own=own):
            o_ref[0] = attend(own)
            score_pages()


def _moba_prompt(qm, kt, vt, q_new, k_pages, page_table):
    nb, s, _ = qm.shape
    blk = MOBA_BLOCK
    assert s % blk == 0 and s // blk <= LANES
    n_pairs = D_MOBA // LANES
    n_tiles = s // blk
    nb8 = -(-n_tiles // 8) * 8
    nb_new, t, _ = q_new.shape
    n_pages = page_table.shape[1]
    pps = n_pages // n_tiles
    assert nb_new == nb * n_pairs and pps * n_tiles == n_pages and (pps * PAGE_SIZE) % MOBA_BLOCK == 0
    rows = MOBA_HEADS * t
    keys = pps * PAGE_SIZE
    qmap = lambda b, hp, i, pt: (b, i, hp)
    kmap = lambda b, hp, i, pt: (b, hp, 0)

    def page_spec(p):
        return pl.BlockSpec((1, D_MOBA, PAGE_SIZE),
                            lambda b, hp, i, pt, p=p: (pt[b * n_pairs + hp, i * pps + p], 0, 0))

    return pl.pallas_call(
        functools.partial(_moba_prompt_kernel, pages_per_step=pps),
        out_shape=[jax.ShapeDtypeStruct((nb, s, D_MOBA), F32),
                   jax.ShapeDtypeStruct((nb_new, rows, n_pages * PAGE_SIZE), BF16),
                   jax.ShapeDtypeStruct((nb_new, n_tiles, rows, LANES), F32)],
        grid_spec=pltpu.PrefetchScalarGridSpec(
            num_scalar_prefetch=1, grid=(nb, n_pairs, n_tiles),
            in_specs=[pl.BlockSpec((1, blk, LANES), qmap),
                      pl.BlockSpec((1, LANES, s), kmap),
                      pl.BlockSpec((1, LANES, s), kmap),
                      pl.BlockSpec((1, t, D_MOBA), lambda b, hp, i, pt: (b * n_pairs + hp, 0, 0))]
                     + [page_spec(p) for p in range(pps)],
            out_specs=[pl.BlockSpec((1, blk, LANES), qmap),
                       pl.BlockSpec((1, rows, keys), lambda b, hp, i, pt: (b * n_pairs + hp, 0, i)),
                       pl.BlockSpec((1, 1, rows, LANES), lambda b, hp, i, pt: (b * n_pairs + hp, i, 0, 0))],
            scratch_shapes=[pltpu.VMEM((2, nb8, MOBA_HEAD_DIM), F32)]),
        compiler_params=_cparams(("parallel", "parallel", "arbitrary")),
        name="moba_prompt",
    )(page_table, qm, kt, vt, q_new, *([k_pages] * pps))


def _gla_out(o, gg, ng):
    ms = jnp.mean(o * o, axis=1, keepdims=True)
    return o * lax.rsqrt(ms + RMS_EPS) * ng * _silu(gg)


def _gla_prompt_kernel(q_ref, k_ref, la_ref, v_ref, gg_ref, ng_ref, o_ref, s_ref, st_scr):
    c = GLA_CHUNK
    n_chunks = q_ref.shape[1] // c
    lane = lax.broadcasted_iota(jnp.int32, (1, LANES), 1)
    row = lax.broadcasted_iota(jnp.int32, (c, 1), 0)
    col = lax.broadcasted_iota(jnp.int32, (1, c), 1)
    tril_bf = jnp.where(col <= row, 1.0, 0.0).astype(BF16)
    ng = ng_ref[...]
    st_scr[...] = jnp.zeros_like(st_scr)

    halves = []
    half = c // 2
    while half >= GLA_LEAF:
        halves.append(half)
        half //= 2
    def same_block(size):
        sh = size.bit_length() - 1
        return lax.shift_right_logical(row, sh) == lax.shift_right_logical(col, sh)

    pair_masks = [same_block(2 * hf) & ((row & hf) != 0) & ((col & hf) == 0) for hf in halves]
    leaf_mask = same_block(GLA_LEAF) & (col <= row)

    def rows_of(b, size, pick):
        parts = []
        for n in range(c // size):
            r = pick(n)
            src = b[r:r + 1, :] if r >= 0 else jnp.zeros((1, LANES), F32)
            parts.append(jnp.broadcast_to(src, (size, LANES)))
        return jnp.concatenate(parts, axis=0)

    def body(ci, carry):
        r0 = pl.multiple_of(ci * c, c)
        for hp in range(GLA_HEADS // 2):
            sl = slice(hp * LANES, (hp + 1) * LANES)
            q = q_ref[0, pl.ds(r0, c), sl]
            k = k_ref[0, pl.ds(r0, c), sl]
            b = _dot_exact_lhs(tril_bf, la_ref[0, pl.ds(r0, c), sl])
            b_end = b[c - 1:c, :]
            q_dec = q * jnp.exp(b)
            k_dec = k * jnp.exp(b_end - b)
            e_end = jnp.exp(b_end)
            level_qk = []
            for hf in halves:
                u = jnp.exp(-jnp.abs(b - rows_of(b, 2 * hf, lambda n: n * 2 * hf + hf - 1)))
                level_qk.append((q * u, k * u))
            b_leaf = b - rows_of(b, GLA_LEAF, lambda n: n * GLA_LEAF - 1)
            q_leaf, k_leaf = q * jnp.exp(b_leaf), k * jnp.exp(-b_leaf)
            for hh in range(2):
                h = 2 * hp + hh
                hm = (lane >= GLA_DK * hh) & (lane < GLA_DK * (hh + 1))
                qb = jnp.where(hm, q_dec, 0.0)
                vh = v_ref[0, pl.ds(r0, c), h * GLA_DV:(h + 1) * GLA_DV]
                att = jnp.where(leaf_mask, _dot_nt(jnp.where(hm, q_leaf, 0.0), k_leaf), 0.0)
                for mask, (ql, kl) in zip(pair_masks, level_qk):
                    att = jnp.where(mask, _dot_nt(jnp.where(hm, ql, 0.0), kl), att)
                st = st_scr[h]
                o = _dot(att, vh) + _dot_nt(qb, st)
                st_scr[h] = st * e_end + _dot(vh.T, jnp.where(hm, k_dec, 0.0))
                gg = gg_ref[0, pl.ds(r0, c), h * GLA_DV:(h + 1) * GLA_DV]
                o_ref[0, pl.ds(r0, c), h * GLA_DV:(h + 1) * GLA_DV] = _gla_out(o, gg, ng)
        return carry

    lax.fori_loop(0, n_chunks, body, 0, unroll=4)
    for h in range(GLA_HEADS):
        hh = h % 2
        s_ref[0, 0, h] = st_scr[h].T[hh * GLA_DK:(hh + 1) * GLA_DK, :]


def _gla_prompt(qg, kg, la, vg, gg, norm_g):
    nb, s, _ = qg.shape
    assert s % GLA_CHUNK == 0
    map3 = lambda b: (b, 0, 0)
    return pl.pallas_call(
        _gla_prompt_kernel,
        out_shape=[jax.ShapeDtypeStruct((nb, s, D_GLA), F32),
                   jax.ShapeDtypeStruct((1, nb, GLA_HEADS, GLA_DK, GLA_DV), F32)],
        grid=(nb,),
        in_specs=[pl.BlockSpec((1, s, D_GLA_K), map3),
                  pl.BlockSpec((1, s, D_GLA_K), map3),
                  pl.BlockSpec((1, s, D_GLA_K), map3),
                  pl.BlockSpec((1, s, D_GLA), map3),
                  pl.BlockSpec((1, s, D_GLA), map3),
                  pl.BlockSpec((1, GLA_DV), lambda b: (0, 0))],
        out_specs=[pl.BlockSpec((1, s, D_GLA), map3),
                   pl.BlockSpec((1, 1, GLA_HEADS, GLA_DK, GLA_DV), lambda b: (0, b, 0, 0, 0))],
        scratch_shapes=[pltpu.VMEM((GLA_HEADS, GLA_DV, LANES), F32)],
        compiler_params=_cparams(("parallel",)),
        name="gla_prompt",
    )(qg, kg, la, vg, gg, norm_g.reshape(1, GLA_DV))


def _gla_sample_kernel(q_ref, k_ref, la_ref, v_ref, gg_ref, s0_ref, ng_ref, o_ref, s_ref):
    rows_per_step, t = q_ref.shape[0], q_ref.shape[1]
    lane = lax.broadcasted_iota(jnp.int32, (1, LANES), 1)
    row = lax.broadcasted_iota(jnp.int32, (t, 1), 0)
    ng = ng_ref[...]
    eye = (lax.broadcasted_iota(jnp.int32, (GLA_DK, 1), 0)
           == lax.broadcasted_iota(jnp.int32, (1, GLA_DK), 1))
    zeros_half = jnp.zeros((GLA_DK, GLA_DV), F32)
    for bi in range(rows_per_step):
        for hp in range(GLA_HEADS // 2):
            sl = slice(hp * LANES, (hp + 1) * LANES)
            q = q_ref[bi, :, sl]
            k = k_ref[bi, :, sl]
            b = la_ref[bi, :, sl]
            sh = 1
            while sh < t:
                b = b + jnp.where(row >= sh, pltpu.roll(b, sh, 0), 0.0)
                sh *= 2
            b_end = b[t - 1:t, :]
            e_end = jnp.exp(b_end)
            q_dec = q * jnp.exp(b)
            k_inv = _pad_rows(k * jnp.exp(-b), LANES)
            k_dec = k * jnp.exp(b_end - b)
            for hh in range(2):
                h = 2 * hp + hh
                hm = (lane >= GLA_DK * hh) & (lane < GLA_DK * (hh + 1))
                qb = jnp.where(hm, q_dec, 0.0)
                vh = _pad_rows(v_ref[bi, :, h * GLA_DV:(h + 1) * GLA_DV], LANES)
                s0 = s0_ref[bi, h]
                s0_pad = jnp.concatenate([s0, zeros_half] if hh == 0 else [zeros_half, s0], axis=0)
                att = jnp.where(lane <= row, _dot_nt(qb, k_inv), 0.0)
                o = _dot(att, vh) + _dot(qb, s0_pad)
                gg = gg_ref[bi, :, h * GLA_DV:(h + 1) * GLA_DV]
                o_ref[bi, :, h * GLA_DV:(h + 1) * GLA_DV] = _gla_out(o, gg, ng)
                kd = _pad_rows(jnp.where(hm, k_dec, 0.0), LANES)
                upd = _dot(kd.T, vh)[hh * GLA_DK:(hh + 1) * GLA_DK, :]
                e_h = e_end[:, hh * GLA_DK:(hh + 1) * GLA_DK]
                diag = jnp.where(eye, jnp.broadcast_to(e_h, (GLA_DK, GLA_DK)), 0.0)
                s_ref[0, bi, h] = _dot3(diag, s0) + upd


def _gla_sample(qg, kg, la, vg, gg, s0, norm_g, rows_per_step=4):
    nb, t, _ = qg.shape
    r = rows_per_step
    assert nb % r == 0
    map3 = lambda b: (b, 0, 0)
    return pl.pallas_call(
        _gla_sample_kernel,
        out_shape=[jax.ShapeDtypeStruct((nb, t, D_GLA), F32),
                   jax.ShapeDtypeStruct((1, nb, GLA_HEADS, GLA_DK, GLA_DV), F32)],
        grid=(nb // r,),
        in_specs=[pl.BlockSpec((r, t, D_GLA_K), map3),
                  pl.BlockSpec((r, t, D_GLA_K), map3),
                  pl.BlockSpec((r, t, D_GLA_K), map3),
                  pl.BlockSpec((r, t, D_GLA), map3),
                  pl.BlockSpec((r, t, D_GLA), map3),
                  pl.BlockSpec((r, GLA_HEADS, GLA_DK, GLA_DV), lambda b: (b, 0, 0, 0)),
                  pl.BlockSpec((1, GLA_DV), lambda b: (0, 0))],
        out_specs=[pl.BlockSpec((r, t, D_GLA), map3),
                   pl.BlockSpec((1, r, GLA_HEADS, GLA_DK, GLA_DV), lambda b: (0, b, 0, 0, 0))],
        compiler_params=_cparams(("parallel",)),
        name="gla_sample",
    )(qg, kg, la, vg, gg, s0, norm_g.reshape(1, GLA_DV))


def _block_diag_queries(q):
    lane = lax.broadcasted_iota(jnp.int32, (1, D_MOBA), 1)
    parts = [jnp.where((lane >= MOBA_HEAD_DIM * h) & (lane < MOBA_HEAD_DIM * (h + 1)), q, 0.0)
             for h in range(MOBA_HEADS)]
    return jnp.concatenate(parts, axis=0)


def _moba_attend_kernel(pt_ref, q_ref, g_ref, sc_ref, kn_ref, vn_ref, *refs, n_full):
    del pt_ref
    n = PAGES_PER_STEP
    pages, o_ref = refs[:n], refs[n]
    sel_scr, m_scr, l_scr, acc_scr = refs[n + 1:]
    s = pl.program_id(1)
    t = q_ref.shape[1]
    rows = MOBA_HEADS * t
    keys = n * PAGE_SIZE
    bps = keys // MOBA_BLOCK
    lane = lax.broadcasted_iota(jnp.int32, (1, LANES), 1)
    lane_f = lane.astype(F32)

    @pl.when(s == 0)
    def _():
        gbps = n_full // g_ref.shape[1]
        g = jnp.zeros((rows, LANES), F32)
        for st in range(g_ref.shape[1]):
            g = g + pltpu.roll(g_ref[0, st], st * gbps, 1)
        g = jnp.where(lane < n_full, g, -jnp.inf)
        sel = jnp.zeros((rows, LANES), F32)
        for _ in range(MOBA_TOPK):
            mx = jnp.max(g, axis=1, keepdims=True)
            first = jnp.min(jnp.where(g == mx, lane_f, float(LANES)), axis=1, keepdims=True)
            pick = lane_f == first
            sel = jnp.where(pick, 1.0, sel)
            g = jnp.where(pick, -jnp.inf, g)
        sel_scr[...] = sel
        m_scr[...] = jnp.full(m_scr.shape, NEG_BIG, F32)
        l_scr[...] = jnp.zeros_like(l_scr)
        acc_scr[...] = jnp.zeros_like(acc_scr)

    col = lax.broadcasted_iota(jnp.int32, (1, keys), 1)
    blk_of_col = lax.shift_right_logical(col, MOBA_BLOCK.bit_length() - 1) + s * bps
    expand = jnp.where(lax.broadcasted_iota(jnp.int32, (LANES, 1), 0) == blk_of_col, 1.0, 0.0)
    pmask = _dot(sel_scr[...], expand) > 0.5

    def pv(p):
        p = p.astype(BF16)
        out = jnp.zeros(acc_scr.shape, F32)
        for pg in range(n):
            out += _dot_nt(p[:, pg * PAGE_SIZE:(pg + 1) * PAGE_SIZE], pages[pg][0])
        return out

    _softmax_step(sc_ref[0].astype(F32), pmask, pv, m_scr, l_scr, acc_scr)

    @pl.when(s == pl.num_programs(1) - 1)
    def _():
        qbd = _block_diag_queries(q_ref[0]) * (MOBA_HEAD_DIM ** -0.5)
        row_t = lax.broadcasted_iota(jnp.int32, (rows, 1), 0) & (t - 1)
        s_own = _dot_nt(qbd, _pad_rows(kn_ref[0], LANES))
        v_own = _pad_rows(vn_ref[0], LANES)
        _softmax_step(s_own, lane <= row_t, lambda p: _dot(p, v_own), m_scr, l_scr, acc_scr)
        out = acc_scr[...] / l_scr[...]
        lane_w = lax.broadcasted_iota(jnp.int32, (1, D_MOBA), 1)
        o = jnp.zeros((t, D_MOBA), F32)
        for h in range(MOBA_HEADS):
            hm = (lane_w >= MOBA_HEAD_DIM * h) & (lane_w < MOBA_HEAD_DIM * (h + 1))
            o = o + jnp.where(hm, out[h * t:(h + 1) * t, :], 0.0)
        o_ref[0] = o


def _moba_sample(qm, km, vm, scores, gates, v_pages, page_table):
    nb, t, _ = qm.shape
    n_pages = page_table.shape[1]
    past = n_pages * PAGE_SIZE
    n_full = past // MOBA_BLOCK
    assert past % MOBA_BLOCK == 0 and MOBA_TOPK <= n_full <= LANES and t & (t - 1) == 0
    n = PAGES_PER_STEP
    assert n_pages % n == 0 and (n * PAGE_SIZE) % MOBA_BLOCK == 0 and n_full % gates.shape[1] == 0
    steps = n_pages // n
    keys = n * PAGE_SIZE
    rows = MOBA_HEADS * t

    def page_spec(p):
        return pl.BlockSpec((1, D_MOBA, PAGE_SIZE), lambda b, s, pt, p=p: (pt[b, s * n + p], 0, 0))

    qspec = pl.BlockSpec((1, t, D_MOBA), lambda b, s, pt: (b, 0, 0))
    return pl.pallas_call(
        functools.partial(_moba_attend_kernel, n_full=n_full),
        out_shape=jax.ShapeDtypeStruct((nb, t, D_MOBA), F32),
        grid_spec=pltpu.PrefetchScalarGridSpec(
            num_scalar_prefetch=1, grid=(nb, steps),
            in_specs=[qspec,
                      pl.BlockSpec((1,) + gates.shape[1:], lambda b, s, pt: (b, 0, 0, 0)),
                      pl.BlockSpec((1, rows, keys), lambda b, s, pt: (b, 0, s)),
                      qspec, qspec] + [page_spec(p) for p in range(n)],
            out_specs=qspec,
            scratch_shapes=[pltpu.VMEM((rows, LANES), F32), pltpu.VMEM((rows, 1), F32),
                            pltpu.VMEM((rows, 1), F32), pltpu.VMEM((rows, D_MOBA), F32)]),
        compiler_params=_cparams(("parallel", "arbitrary")),
        name="moba_sample_attend",
    )(page_table, qm, gates, scores, km, vm, *([v_pages] * n))


def kernel(x_prompt, x_sample, cache_k, cache_v, state_gla, page_table, c_prompt, c_sample, w_ada, b_ada, ln_g, ln_b, w_ffn1_in, w_ffn1_out, w_mix_in, w_gk_up, b_gk, gla_norm_g, w_mix_out, w_ffn2_in, w_ffn2_out):
    depth = w_ada.shape[0]
    assert depth == 1, "one decoder layer"
    d = x_prompt.shape[-1]
    alpha = (2.0 * depth) ** 0.25
    nb_p, s_p, _ = x_prompt.shape
    nb_s, t_s, _ = x_sample.shape
    past_len = page_table.shape[1] * PAGE_SIZE

    w1_in, w1_out = w_ffn1_in[0].astype(BF16), w_ffn1_out[0].astype(BF16)
    w2_in, w2_out = w_ffn2_in[0].astype(BF16), w_ffn2_out[0].astype(BF16)
    w_mix = w_mix_in[0]
    w_main = w_mix[:, :D_MIX_MAIN].astype(BF16)
    w_rg = jnp.pad(w_mix[:, D_MIX_MAIN:], ((0, 0), (0, LANES - GLA_GATE_RANK))).astype(BF16)
    w_gk = jnp.pad(w_gk_up[0], ((0, LANES - GLA_GATE_RANK), (0, 0)))
    w_mo = w_mix_out[0].astype(BF16)

    c_all = jnp.concatenate([c_sample, c_prompt], axis=0)
    m = _ada_modulation(c_all, w_ada[0], b_ada[0])
    m4 = m.reshape(3 * N_SUBLAYERS, c_all.shape[0], 1, d)

    def first_half(x, m_row0, pos_base, kv_transposed):
        x = _ffn(x, m4, m_row0, 0, w1_in, w1_out, ln_g[0, 0], ln_b[0, 0], alpha)
        return x, _mixer_in(x, m4, m_row0, w_main, w_rg, w_gk, b_gk[0], pos_base, kv_transposed)

    def second_half(x, m_row0, o_moba, o_gla):
        return _ffn(x, m4, m_row0, 2, w2_in, w2_out, ln_g[0, 2], ln_b[0, 2], alpha,
                    mixer_out=(o_moba, o_gla, w_mo, ln_g[0, 1], ln_b[0, 1]))

    k_pages = jnp.transpose(cache_k[0], (0, 2, 3, 1)).reshape(cache_k.shape[1], D_MOBA, PAGE_SIZE)
    v_pages = jnp.transpose(cache_v[0], (0, 2, 3, 1)).reshape(cache_v.shape[1], D_MOBA, PAGE_SIZE)

    xs, (qm_s, ks, vs, qg_s, kg_s, vg_s, gg_s, la_s) = first_half(x_sample, 0, past_len, False)
    xp, (qm_p, kp_t, vp_t, qg_p, kg_p, vg_p, gg_p, la_p) = first_half(x_prompt, nb_s, 0, True)
    o_moba_p, scores, gates = _moba_prompt(qm_p, kp_t, vp_t, qm_s, k_pages, page_table)
    o_gla_p, sp = _gla_prompt(qg_p, kg_p, la_p, vg_p, gg_p, gla_norm_g[0])
    yp = second_half(xp, nb_s, o_moba_p, o_gla_p)
    o_moba_s = _moba_sample(qm_s, ks, vs, scores, gates, v_pages, page_table)
    o_gla_s, ss = _gla_sample(qg_s, kg_s, la_s, vg_s, gg_s, state_gla[0], gla_norm_g[0])
    ys = second_half(xs, 0, o_moba_s, o_gla_s)

    def rows_major(a_t):
        a = a_t.reshape(nb_p, MOBA_HEADS, MOBA_HEAD_DIM, s_p)
        return jnp.transpose(a, (0, 3, 1, 2))[None]

    def heads(a):
        return a.reshape(1, nb_s, t_s, MOBA_HEADS, MOBA_HEAD_DIM)

    return (yp, ys, rows_major(kp_t), rows_major(vp_t), sp, heads(ks), heads(vs), ss)
```

```python
import functools

import jax
import jax.numpy as jnp
from jax import lax
from jax.experimental import pallas as pl
from jax.experimental.pallas import tpu as pltpu

F32 = jnp.float32
BF16 = jnp.bfloat16

PAGE_SIZE = 128
MOBA_HEADS = 8
MOBA_HEAD_DIM = 64
D_MOBA = MOBA_HEADS * MOBA_HEAD_DIM
MOBA_BLOCK = 256
MOBA_TOPK = 3
ROPE_THETA = 500000.0
ROPE_DIMS = MOBA_HEAD_DIM // 4
ROPE_HALF = ROPE_DIMS // 2
GLA_HEADS = 4
GLA_DK = 64
GLA_DV = 128
D_GLA_K = GLA_HEADS * GLA_DK
D_GLA = GLA_HEADS * GLA_DV
GLA_GATE_RANK = 16
GLA_GATE_NORM = 16.0
D_MIX_MAIN = 3 * D_MOBA + 2 * D_GLA_K + 2 * D_GLA
N_SUBLAYERS = 3
LN_EPS = 1e-5
RMS_EPS = 1e-6

LANES = 128
NEG_BIG = -1e30
VMEM_LIMIT = 56 * 1024 * 1024

FFN_ROWS = 512
FFN_CHUNKS = 1
GLA_CHUNK = 128
GLA_LEAF = 32
PAGES_PER_STEP = 64


def _cparams(sem):
    return pltpu.CompilerParams(dimension_semantics=sem, vmem_limit_bytes=VMEM_LIMIT)


def _dot(a, b):
    return jnp.dot(a.astype(BF16), b.astype(BF16), preferred_element_type=F32)


def _dot_nt(a, b):
    return lax.dot_general(a.astype(BF16), b.astype(BF16), (((1,), (1,)), ((), ())),
                           preferred_element_type=F32)


def _split2(x):
    hi = x.astype(BF16)
    lo = (x - hi.astype(F32)).astype(BF16)
    return hi, lo


def _dot3(a, b, nt=False):
    d = _dot_nt if nt else _dot
    ah, al = _split2(a)
    bh, bl = _split2(b)
    return d(ah, bh) + (d(ah, bl) + d(al, bh))


def _dot_exact_lhs(lhs_bf16, x):
    hi = x.astype(BF16)
    r1 = x - hi.astype(F32)
    mid = r1.astype(BF16)
    lo = (r1 - mid.astype(F32)).astype(BF16)
    f = functools.partial(jnp.dot, lhs_bf16, preferred_element_type=F32)
    return f(hi) + (f(mid) + f(lo))


def _silu(x):
    return x * jax.nn.sigmoid(x)


def _layer_norm(y, g, b):
    mu = jnp.mean(y, axis=-1, keepdims=True)
    yc = y - mu
    var = jnp.mean(yc * yc, axis=-1, keepdims=True)
    return yc * lax.rsqrt(var + LN_EPS) * g + b


def _pad_rows(x, rows):
    return jnp.concatenate([x, jnp.zeros((rows - x.shape[0], x.shape[1]), x.dtype)], axis=0)


def _ada_kernel(c_ref, w_ref, b_ref, o_ref):
    o_ref[0] = _dot3(_silu(c_ref[...]), w_ref[...]) + b_ref[...]


def _ada_modulation(c_all, w_ada, b_ada):
    nb, d = c_all.shape
    n_out = w_ada.shape[1] // d
    return pl.pallas_call(
        _ada_kernel,
        out_shape=jax.ShapeDtypeStruct((n_out, nb, d), F32),
        grid=(n_out,),
        in_specs=[pl.BlockSpec((nb, d), lambda n: (0, 0)),
                  pl.BlockSpec((d, d), lambda n: (0, n)),
                  pl.BlockSpec((1, d), lambda n: (0, n))],
        out_specs=pl.BlockSpec((1, nb, d), lambda n: (n, 0, 0)),
        compiler_params=_cparams(("arbitrary",)),
        name="ada_modulation",
    )(c_all, w_ada, b_ada.reshape(1, -1))


def _ffn_kernel(*refs, alpha, fused_mixer_out, n_chunks):
    if fused_mixer_out:
        x_ref, om_ref, og_ref, mm_ref, wmo_ref, gm_ref, bm_ref = refs[:7]
        m_ref, wa_ref, wu_ref, wo_ref, g_ref, b_ref, o_ref, h_scr, acc_scr, res_scr = refs[7:]
    else:
        x_ref, m_ref, wa_ref, wu_ref, wo_ref, g_ref, b_ref, o_ref, h_scr, acc_scr = refs
        res_scr = x_ref
    j = pl.program_id(1)
    bb, ts, d = x_ref.shape
    tm = bb * ts

    def modulated_input():
        x = x_ref[...]
        if fused_mixer_out:
            mix = (_dot(om_ref[...].reshape(tm, D_MOBA), wmo_ref[0:D_MOBA, :])
                   + _dot(og_ref[...].reshape(tm, D_GLA), wmo_ref[D_MOBA:D_MOBA + D_GLA, :]))
            x = _layer_norm(alpha * x + (1.0 + mm_ref[2]) * mix.reshape(bb, ts, d), gm_ref[...], bm_ref[...])
            res_scr[...] = x
        h = (x * (1.0 + m_ref[1]) + m_ref[0]).reshape(tm, d).astype(BF16)
        h_scr[...] = h
        return h

    def chunk(h):
        a = jnp.dot(h, wa_ref[...], preferred_element_type=F32)
        u = jnp.dot(h, wu_ref[...], preferred_element_type=F32)
        t = (_silu(a) * u).astype(BF16)
        return jnp.dot(t, wo_ref[...], preferred_element_type=F32)

    def post_norm(acc):
        y = alpha * res_scr[...] + (0.5 * (1.0 + m_ref[2])) * acc.reshape(bb, ts, d)
        o_ref[...] = _layer_norm(y, g_ref[...], b_ref[...])

    @pl.when(j == 0)
    def _():
        acc = chunk(modulated_input())
        if n_chunks == 1:
            post_norm(acc)
        else:
            acc_scr[...] = acc

    if n_chunks > 2:
        @pl.when((j > 0) & (j < n_chunks - 1))
        def _():
            acc_scr[...] += chunk(h_scr[...])

    if n_chunks > 1:
        @pl.when(j == n_chunks - 1)
        def _():
            post_norm(acc_scr[...] + chunk(h_scr[...]))


def _row_tiling(x):
    nb, s, _ = x.shape
    if s >= FFN_ROWS:
        assert s % FFN_ROWS == 0
        return 1, FFN_ROWS
    assert s % 8 == 0 and FFN_ROWS % s == 0
    bb = min(nb, FFN_ROWS // s)
    assert nb % bb == 0
    return bb, s


def _ffn(x, m4, m_row0, sub, w_in, w_out, ln_g, ln_b, alpha, mixer_out=None, n_chunks=FFN_CHUNKS):
    nb, s, d = x.shape
    bb, ts = _row_tiling(x)
    tpb = s // ts
    d_ff = w_out.shape[0]
    ck = d_ff // n_chunks
    assert ck * n_chunks == d_ff and ck % LANES == 0 and m_row0 % bb == 0
    grid = ((nb // bb) * tpb, n_chunks)
    xmap = lambda i, j: (i // tpb, i % tpb, 0)
    const2 = lambda i, j: (0, 0)

    def mspec(k):
        return pl.BlockSpec((3, bb, 1, d), lambda i, j: (k, m_row0 // bb + i // tpb, 0, 0))

    vec = pl.BlockSpec((1, d), const2)
    in_specs = [pl.BlockSpec((bb, ts, d), xmap)]
    args = [x]
    scratch = [pltpu.VMEM((bb * ts, d), BF16), pltpu.VMEM((bb * ts, d), F32)]
    if mixer_out is not None:
        o_moba, o_gla, w_mo, g_mo, b_mo = mixer_out
        in_specs += [pl.BlockSpec((bb, ts, D_MOBA), xmap), pl.BlockSpec((bb, ts, D_GLA), xmap),
                     mspec(1), pl.BlockSpec(w_mo.shape, const2), vec, vec]
        args += [o_moba, o_gla, m4, w_mo, g_mo.reshape(1, d), b_mo.reshape(1, d)]
        scratch.append(pltpu.VMEM((bb, ts, d), F32))
    in_specs += [mspec(sub),
                 pl.BlockSpec((d, ck), lambda i, j: (0, j)),
                 pl.BlockSpec((d, ck), lambda i, j: (0, j + n_chunks)),
                 pl.BlockSpec((ck, d), lambda i, j: (j, 0)),
                 vec, vec]
    args += [m4, w_in, w_in, w_out, ln_g.reshape(1, d), ln_b.reshape(1, d)]
    return pl.pallas_call(
        functools.partial(_ffn_kernel, alpha=alpha, fused_mixer_out=mixer_out is not None,
                          n_chunks=n_chunks),
        out_shape=jax.ShapeDtypeStruct(x.shape, F32),
        grid=grid,
        in_specs=in_specs,
        out_specs=pl.BlockSpec((bb, ts, d), xmap),
        scratch_shapes=scratch,
        compiler_params=_cparams(("parallel", "arbitrary")),
        name="ffn_postnorm",
    )(*args)


def _rope_table_kernel(o_ref, *, pos_base):
    tt = o_ref.shape[1]
    lane = lax.broadcasted_iota(jnp.int32, (1, LANES), 1)
    fi = (lane & (ROPE_HALF - 1)).astype(F32)
    inv = jnp.power(jnp.full((1, LANES), ROPE_THETA, F32), -fi / ROPE_HALF)
    row = lax.broadcasted_iota(jnp.int32, (tt, 1), 0)
    pos = (pos_base + pl.program_id(0) * tt + row).astype(F32)
    ang = pos * inv
    cos = jnp.cos(ang)
    sin = jnp.sin(ang)
    l64 = lane & (MOBA_HEAD_DIM - 1)
    o_ref[0] = jnp.where(l64 < ROPE_DIMS, cos, 1.0)
    o_ref[1] = jnp.where(l64 < ROPE_HALF, -sin, 0.0)
    o_ref[2] = jnp.where((l64 >= ROPE_HALF) & (l64 < ROPE_DIMS), sin, 0.0)


def _rope_tables(n_pos, tile, pos_base):
    return pl.pallas_call(
        functools.partial(_rope_table_kernel, pos_base=pos_base),
        out_shape=jax.ShapeDtypeStruct((3, n_pos, LANES), F32),
        grid=(n_pos // tile,),
        in_specs=[],
        out_specs=pl.BlockSpec((3, tile, LANES), lambda i: (0, i, 0)),
        compiler_params=_cparams(("arbitrary",)),
        name="rope_tables",
    )()


def _mixin_kernel(x_ref, m_ref, tab_ref, w_ref, wrg_ref, wgk_ref, bgk_ref,
                  qm_ref, km_ref, vm_ref, qg_ref, kg_ref, vg_ref, gg_ref, la_ref,
                  *, kv_transposed):
    bb, ts, d = x_ref.shape
    tm = bb * ts
    h = (x_ref[...] * (1.0 + m_ref[1]) + m_ref[0]).reshape(tm, d).astype(BF16)
    p = jnp.dot(h, w_ref[...], preferred_element_type=F32)
    c_tab, s_lo, s_hi = tab_ref[0], tab_ref[1], tab_ref[2]

    def rope(x):
        slabs = []
        for s in range(x.shape[1] // LANES):
            xs = x[:, s * LANES:(s + 1) * LANES]
            r = (xs.reshape(bb, ts, LANES) * c_tab
                 + pltpu.roll(xs, LANES - ROPE_HALF, 1).reshape(bb, ts, LANES) * s_lo
                 + pltpu.roll(xs, ROPE_HALF, 1).reshape(bb, ts, LANES) * s_hi)
            slabs.append(r.reshape(tm, LANES))
        return jnp.concatenate(slabs, axis=1)

    def store(ref, x):
        ref[...] = x.reshape(ref.shape)

    def store_kv(ref, x):
        if kv_transposed:
            ref[0] = x.T
        else:
            store(ref, x)

    o = 0
    store(qm_ref, rope(p[:, o:o + D_MOBA])); o += D_MOBA
    store_kv(km_ref, rope(p[:, o:o + D_MOBA])); o += D_MOBA
    store_kv(vm_ref, p[:, o:o + D_MOBA]); o += D_MOBA
    store(qg_ref, p[:, o:o + D_GLA_K] * (GLA_DK ** -0.5)); o += D_GLA_K
    store(kg_ref, p[:, o:o + D_GLA_K]); o += D_GLA_K
    store(vg_ref, p[:, o:o + D_GLA]); o += D_GLA
    store(gg_ref, p[:, o:o + D_GLA]); o += D_GLA

    rg = jnp.dot(h, wrg_ref[...], preferred_element_type=F32)
    z = _dot3(rg, wgk_ref[...]) + bgk_ref[...]
    log_sig = jnp.minimum(z, 0.0) - jnp.log1p(jnp.exp(-jnp.abs(z)))
    store(la_ref, log_sig / GLA_GATE_NORM)


def _mixer_in(x, m4, m_row0, w_main, w_rg, w_gk, b_gk, pos_base, kv_transposed):
    nb, s, d = x.shape
    bb, ts = _row_tiling(x)
    tpb = s // ts
    assert m_row0 % bb == 0 and (bb == 1 or not kv_transposed)
    grid = ((nb // bb) * tpb,)
    xmap = lambda i: (i // tpb, i % tpb, 0)
    tmap = lambda i: (i // tpb, 0, i % tpb)
    mmap = lambda i: (1, m_row0 // bb + i // tpb, 0, 0)
    const2 = lambda i: (0, 0)
    widths = (D_MOBA, D_MOBA, D_MOBA, D_GLA_K, D_GLA_K, D_GLA, D_GLA, D_GLA_K)
    out_shape = [jax.ShapeDtypeStruct((nb, s, w), F32) for w in widths]
    out_specs = [pl.BlockSpec((bb, ts, w), xmap) for w in widths]
    if kv_transposed:
        for n in (1, 2):
            out_shape[n] = jax.ShapeDtypeStruct((nb, D_MOBA, s), F32)
            out_specs[n] = pl.BlockSpec((1, D_MOBA, ts), tmap)
    tables = _rope_tables(s, ts, pos_base)
    return pl.pallas_call(
        functools.partial(_mixin_kernel, kv_transposed=kv_transposed),
        out_shape=out_shape,
        grid=grid,
        in_specs=[pl.BlockSpec((bb, ts, d), xmap),
                  pl.BlockSpec((3, bb, 1, d), mmap),
                  pl.BlockSpec((3, ts, LANES), lambda i: (0, i % tpb, 0)),
                  pl.BlockSpec(w_main.shape, const2),
                  pl.BlockSpec(w_rg.shape, const2),
                  pl.BlockSpec(w_gk.shape, const2),
                  pl.BlockSpec((1, D_GLA_K), const2)],
        out_specs=out_specs,
        compiler_params=_cparams(("parallel",)),
        name="mixer_in",
    )(x, m4, tables, w_main, w_rg, w_gk, b_gk.reshape(1, -1))


def _softmax_step(s, pmask, pv, m_scr, l_scr, acc_scr):
    m_old = m_scr[...]
    m_new = jnp.maximum(m_old, jnp.max(jnp.where(pmask, s, NEG_BIG), axis=1, keepdims=True))
    p = jnp.where(pmask, jnp.exp(s - m_new), 0.0)
    alpha = jnp.exp(m_old - m_new)
    l_scr[...] = alpha * l_scr[...] + jnp.sum(p, axis=1, keepdims=True)
    acc_scr[...] = alpha * acc_scr[...] + pv(p)
    m_scr[...] = m_new


def _moba_prompt_kernel(pt_ref, q_ref, kt_ref, vt_ref, qs_ref, *refs, pages_per_step):
    del pt_ref
    pages = refs[:pages_per_step]
    o_ref, sc_ref, g_ref, kmean_scr = refs[pages_per_step:]
    i = pl.program_id(2)
    blk = q_ref.shape[1]
    s_len = kt_ref.shape[2]
    n_blocks = s_len // blk
    nb8 = kmean_scr.shape[1]
    hd = MOBA_HEAD_DIM
    blk_shift = blk.bit_length() - 1
    blk_row = lax.broadcasted_iota(jnp.int32, (nb8, 1), 0)

    @pl.when(i == 0)
    def _():
        blk_of_key = lax.shift_right_logical(lax.broadcasted_iota(jnp.int32, (1, s_len), 1), blk_shift)
        pool = jnp.where(blk_row == blk_of_key, 1.0 / blk, 0.0).astype(BF16)
        for h in range(2):
            kth = kt_ref[0, h * hd:(h + 1) * hd, :]
            hi = kth.astype(BF16)
            r1 = kth - hi.astype(F32)
            mid = r1.astype(BF16)
            lo = (r1 - mid.astype(F32)).astype(BF16)
            kmean_scr[h] = _dot_nt(pool, hi) + (_dot_nt(pool, mid) + _dot_nt(pool, lo))

    row = lax.broadcasted_iota(jnp.int32, (blk, 1), 0)
    col = lax.broadcasted_iota(jnp.int32, (1, blk), 1)
    causal = col <= row

    def picked_blocks(qh, h, own):
        gt = jnp.where(blk_row < own, _dot3(kmean_scr[h], qh, nt=True), -jnp.inf)
        rank = jnp.zeros(gt.shape, F32)
        for m in range(own):
            gm = gt[m:m + 1, :]
            beats = (gm > gt) | ((gm == gt) & (m < blk_row))
            rank += jnp.where(beats, 1.0, 0.0)
        sel_t = jnp.where((blk_row < own) & (rank < MOBA_TOPK), 1.0, 0.0)
        return _pad_rows(sel_t, LANES).T

    def attend(own):
        q = q_ref[0]
        n = (own + 1) * blk
        outs = []
        for h in range(2):
            qh = q[:, h * hd:(h + 1) * hd]
            s = _dot((qh * (hd ** -0.5)).astype(BF16), kt_ref[0, h * hd:(h + 1) * hd, 0:n])
            if own <= MOBA_TOPK:
                parts = [s[:, 0:own * blk]] if own else []
            else:
                sel = picked_blocks(qh, h, own)
                parts = []
                for j in range(own):
                    picked = jnp.broadcast_to(sel[:, j:j + 1], (blk, blk)) > 0.5
                    parts.append(jnp.where(picked, s[:, j * blk:(j + 1) * blk], -jnp.inf))
            parts.append(jnp.where(causal, s[:, own * blk:n], -jnp.inf))
            sm = jnp.concatenate(parts, axis=1)
            p = jnp.exp(sm - jnp.max(sm, axis=1, keepdims=True))
            l = jnp.sum(p, axis=1, keepdims=True)
            outs.append(_dot_nt(p, vt_ref[0, h * hd:(h + 1) * hd, 0:n]) / l)
        return jnp.concatenate(outs, axis=1)

    def score_pages():
        qbd = (_block_diag_queries(qs_ref[0]) * (hd ** -0.5)).astype(BF16)
        lane = lax.broadcasted_iota(jnp.int32, (1, LANES), 1)
        ppb = MOBA_BLOCK // PAGE_SIZE
        g = jnp.zeros((qbd.shape[0], LANES), F32)
        for p in range(pages_per_step):
            s = _dot(qbd, pages[p][0])
            sc_ref[0, :, p * PAGE_SIZE:(p + 1) * PAGE_SIZE] = s.astype(sc_ref.dtype)
            psum = jnp.sum(s, axis=1, keepdims=True)
            bs = psum if p % ppb == 0 else bs + psum
            if p % ppb == ppb - 1:
                g = jnp.where(lane == p // ppb, bs, g)
        g_ref[0, 0] = g

    for own in range(n_blocks):
        @pl.when(i == own)
        def _(own=own):
            o_ref[0] = attend(own)
            score_pages()


def _moba_prompt(qm, kt, vt, q_new, k_pages, page_table):
    nb, s, _ = qm.shape
    blk = MOBA_BLOCK
    assert s % blk == 0 and s // blk <= LANES
    n_pairs = D_MOBA // LANES
    n_tiles = s // blk
    nb8 = -(-n_tiles // 8) * 8
    nb_new, t, _ = q_new.shape
    n_pages = page_table.shape[1]
    pps = n_pages // n_tiles
    assert nb_new == nb * n_pairs and pps * n_tiles == n_pages and (pps * PAGE_SIZE) % MOBA_BLOCK == 0
    rows = MOBA_HEADS * t
    keys = pps * PAGE_SIZE
    qmap = lambda b, hp, i, pt: (b, i, hp)
    kmap = lambda b, hp, i, pt: (b, hp, 0)

    def page_spec(p):
        return pl.BlockSpec((1, D_MOBA, PAGE_SIZE),
                            lambda b, hp, i, pt, p=p: (pt[b * n_pairs + hp, i * pps + p], 0, 0))

    return pl.pallas_call(
        functools.partial(_moba_prompt_kernel, pages_per_step=pps),
        out_shape=[jax.ShapeDtypeStruct((nb, s, D_MOBA), F32),
                   jax.ShapeDtypeStruct((nb_new, rows, n_pages * PAGE_SIZE), BF16),
                   jax.ShapeDtypeStruct((nb_new, n_tiles, rows, LANES), F32)],
        grid_spec=pltpu.PrefetchScalarGridSpec(
            num_scalar_prefetch=1, grid=(nb, n_pairs, n_tiles),
            in_specs=[pl.BlockSpec((1, blk, LANES), qmap),
                      pl.BlockSpec((1, LANES, s), kmap),
                      pl.BlockSpec((1, LANES, s), kmap),
                      pl.BlockSpec((1, t, D_MOBA), lambda b, hp, i, pt: (b * n_pairs + hp, 0, 0))]
                     + [page_spec(p) for p in range(pps)],
            out_specs=[pl.BlockSpec((1, blk, LANES), qmap),
                       pl.BlockSpec((1, rows, keys), lambda b, hp, i, pt: (b * n_pairs + hp, 0, i)),
                       pl.BlockSpec((1, 1, rows, LANES), lambda b, hp, i, pt: (b * n_pairs + hp, i, 0, 0))],
            scratch_shapes=[pltpu.VMEM((2, nb8, MOBA_HEAD_DIM), F32)]),
        compiler_params=_cparams(("parallel", "parallel", "arbitrary")),
        name="moba_prompt",
    )(page_table, qm, kt, vt, q_new, *([k_pages] * pps))


def _gla_out(o, gg, ng):
    ms = jnp.mean(o * o, axis=1, keepdims=True)
    return o * lax.rsqrt(ms + RMS_EPS) * ng * _silu(gg)


def _gla_prompt_kernel(q_ref, k_ref, la_ref, v_ref, gg_ref, ng_ref, o_ref, s_ref, st_scr):
    c = GLA_CHUNK
    n_chunks = q_ref.shape[1] // c
    lane = lax.broadcasted_iota(jnp.int32, (1, LANES), 1)
    row = lax.broadcasted_iota(jnp.int32, (c, 1), 0)
    col = lax.broadcasted_iota(jnp.int32, (1, c), 1)
    tril_bf = jnp.where(col <= row, 1.0, 0.0).astype(BF16)
    ng = ng_ref[...]
    st_scr[...] = jnp.zeros_like(st_scr)

    halves = []
    half = c // 2
    while half >= GLA_LEAF:
        halves.append(half)
        half //= 2
    def same_block(size):
        sh = size.bit_length() - 1
        return lax.shift_right_logical(row, sh) == lax.shift_right_logical(col, sh)

    pair_masks = [same_block(2 * hf) & ((row & hf) != 0) & ((col & hf) == 0) for hf in halves]
    leaf_mask = same_block(GLA_LEAF) & (col <= row)

    def rows_of(b, size, pick):
        parts = []
        for n in range(c // size):
            r = pick(n)
            src = b[r:r + 1, :] if r >= 0 else jnp.zeros((1, LANES), F32)
            parts.append(jnp.broadcast_to(src, (size, LANES)))
        return jnp.concatenate(parts, axis=0)

    def body(ci, carry):
        r0 = pl.multiple_of(ci * c, c)
        for hp in range(GLA_HEADS // 2):
            sl = slice(hp * LANES, (hp + 1) * LANES)
            q = q_ref[0, pl.ds(r0, c), sl]
            k = k_ref[0, pl.ds(r0, c), sl]
            b = _dot_exact_lhs(tril_bf, la_ref[0, pl.ds(r0, c), sl])
            b_end = b[c - 1:c, :]
            q_dec = q * jnp.exp(b)
            k_dec = k * jnp.exp(b_end - b)
            e_end = jnp.exp(b_end)
            level_qk = []
            for hf in halves:
                u = jnp.exp(-jnp.abs(b - rows_of(b, 2 * hf, lambda n: n * 2 * hf + hf - 1)))
                level_qk.append((q * u, k * u))
            b_leaf = b - rows_of(b, GLA_LEAF, lambda n: n * GLA_LEAF - 1)
            q_leaf, k_leaf = q * jnp.exp(b_leaf), k * jnp.exp(-b_leaf)
            for hh in range(2):
                h = 2 * hp + hh
                hm = (lane >= GLA_DK * hh) & (lane < GLA_DK * (hh + 1))
                qb = jnp.where(hm, q_dec, 0.0)
                vh = v_ref[0, pl.ds(r0, c), h * GLA_DV:(h + 1) * GLA_DV]
                att = jnp.where(leaf_mask, _dot_nt(jnp.where(hm, q_leaf, 0.0), k_leaf), 0.0)
                for mask, (ql, kl) in zip(pair_masks, level_qk):
                    att = jnp.where(mask, _dot_nt(jnp.where(hm, ql, 0.0), kl), att)
                st = st_scr[h]
                o = _dot(att, vh) + _dot_nt(qb, st)
                st_scr[h] = st * e_end + _dot(vh.T, jnp.where(hm, k_dec, 0.0))
                gg = gg_ref[0, pl.ds(r0, c), h * GLA_DV:(h + 1) * GLA_DV]
                o_ref[0, pl.ds(r0, c), h * GLA_DV:(h + 1) * GLA_DV] = _gla_out(o, gg, ng)
        return carry

    lax.fori_loop(0, n_chunks, body, 0, unroll=4)
    for h in range(GLA_HEADS):
        hh = h % 2
        s_ref[0, 0, h] = st_scr[h].T[hh * GLA_DK:(hh + 1) * GLA_DK, :]


def _gla_prompt(qg, kg, la, vg, gg, norm_g):
    nb, s, _ = qg.shape
    assert s % GLA_CHUNK == 0
    map3 = lambda b: (b, 0, 0)
    return pl.pallas_call(
        _gla_prompt_kernel,
        out_shape=[jax.ShapeDtypeStruct((nb, s, D_GLA), F32),
                   jax.ShapeDtypeStruct((1, nb, GLA_HEADS, GLA_DK, GLA_DV), F32)],
        grid=(nb,),
        in_specs=[pl.BlockSpec((1, s, D_GLA_K), map3),
                  pl.BlockSpec((1, s, D_GLA_K), map3),
                  pl.BlockSpec((1, s, D_GLA_K), map3),
                  pl.BlockSpec((1, s, D_GLA), map3),
                  pl.BlockSpec((1, s, D_GLA), map3),
                  pl.BlockSpec((1, GLA_DV), lambda b: (0, 0))],
        out_specs=[pl.BlockSpec((1, s, D_GLA), map3),
                   pl.BlockSpec((1, 1, GLA_HEADS, GLA_DK, GLA_DV), lambda b: (0, b, 0, 0, 0))],
        scratch_shapes=[pltpu.VMEM((GLA_HEADS, GLA_DV, LANES), F32)],
        compiler_params=_cparams(("parallel",)),
        name="gla_prompt",
    )(qg, kg, la, vg, gg, norm_g.reshape(1, GLA_DV))


def _gla_sample_kernel(q_ref, k_ref, la_ref, v_ref, gg_ref, s0_ref, ng_ref, o_ref, s_ref):
    rows_per_step, t = q_ref.shape[0], q_ref.shape[1]
    lane = lax.broadcasted_iota(jnp.int32, (1, LANES), 1)
    row = lax.broadcasted_iota(jnp.int32, (t, 1), 0)
    ng = ng_ref[...]
    eye = (lax.broadcasted_iota(jnp.int32, (GLA_DK, 1), 0)
           == lax.broadcasted_iota(jnp.int32, (1, GLA_DK), 1))
    zeros_half = jnp.zeros((GLA_DK, GLA_DV), F32)
    for bi in range(rows_per_step):
        for hp in range(GLA_HEADS // 2):
            sl = slice(hp * LANES, (hp + 1) * LANES)
            q = q_ref[bi, :, sl]
            k = k_ref[bi, :, sl]
            b = la_ref[bi, :, sl]
            sh = 1
            while sh < t:
                b = b + jnp.where(row >= sh, pltpu.roll(b, sh, 0), 0.0)
                sh *= 2
            b_end = b[t - 1:t, :]
            e_end = jnp.exp(b_end)
            q_dec = q * jnp.exp(b)
            k_inv = _pad_rows(k * jnp.exp(-b), LANES)
            k_dec = k * jnp.exp(b_end - b)
            for hh in range(2):
                h = 2 * hp + hh
                hm = (lane >= GLA_DK * hh) & (lane < GLA_DK * (hh + 1))
                qb = jnp.where(hm, q_dec, 0.0)
                vh = _pad_rows(v_ref[bi, :, h * GLA_DV:(h + 1) * GLA_DV], LANES)
                s0 = s0_ref[bi, h]
                s0_pad = jnp.concatenate([s0, zeros_half] if hh == 0 else [zeros_half, s0], axis=0)
                att = jnp.where(lane <= row, _dot_nt(qb, k_inv), 0.0)
                o = _dot(att, vh) + _dot(qb, s0_pad)
                gg = gg_ref[bi, :, h * GLA_DV:(h + 1) * GLA_DV]
                o_ref[bi, :, h * GLA_DV:(h + 1) * GLA_DV] = _gla_out(o, gg, ng)
                kd = _pad_rows(jnp.where(hm, k_dec, 0.0), LANES)
                upd = _dot(kd.T, vh)[hh * GLA_DK:(hh + 1) * GLA_DK, :]
                e_h = e_end[:, hh * GLA_DK:(hh + 1) * GLA_DK]
                diag = jnp.where(eye, jnp.broadcast_to(e_h, (GLA_DK, GLA_DK)), 0.0)
                s_ref[0, bi, h] = _dot3(diag, s0) + upd


def _gla_sample(qg, kg, la, vg, gg, s0, norm_g, rows_per_step=4):
    nb, t, _ = qg.shape
    r = rows_per_step
    assert nb % r == 0
    map3 = lambda b: (b, 0, 0)
    return pl.pallas_call(
        _gla_sample_kernel,
        out_shape=[jax.ShapeDtypeStruct((nb, t, D_GLA), F32),
                   jax.ShapeDtypeStruct((1, nb, GLA_HEADS, GLA_DK, GLA_DV), F32)],
        grid=(nb // r,),
        in_specs=[pl.BlockSpec((r, t, D_GLA_K), map3),
                  pl.BlockSpec((r, t, D_GLA_K), map3),
                  pl.BlockSpec((r, t, D_GLA_K), map3),
                  pl.BlockSpec((r, t, D_GLA), map3),
                  pl.BlockSpec((r, t, D_GLA), map3),
                  pl.BlockSpec((r, GLA_HEADS, GLA_DK, GLA_DV), lambda b: (b, 0, 0, 0)),
                  pl.BlockSpec((1, GLA_DV), lambda b: (0, 0))],
        out_specs=[pl.BlockSpec((r, t, D_GLA), map3),
                   pl.BlockSpec((1, r, GLA_HEADS, GLA_DK, GLA_DV), lambda b: (0, b, 0, 0, 0))],
        compiler_params=_cparams(("parallel",)),
        name="gla_sample",
    )(qg, kg, la, vg, gg, s0, norm_g.reshape(1, GLA_DV))


def _block_diag_queries(q):
    lane = lax.broadcasted_iota(jnp.int32, (1, D_MOBA), 1)
    parts = [jnp.where((lane >= MOBA_HEAD_DIM * h) & (lane < MOBA_HEAD_DIM * (h + 1)), q, 0.0)
             for h in range(MOBA_HEADS)]
    return jnp.concatenate(parts, axis=0)


def _moba_attend_kernel(pt_ref, q_ref, g_ref, sc_ref, kn_ref, vn_ref, *refs, n_full, n_steps):
    del pt_ref
    n = PAGES_PER_STEP
    pages, o_ref = refs[:n], refs[n]
    sel_scr, m_scr, l_scr, acc_scr = refs[n + 1:]
    s = pl.program_id(1)
    t = q_ref.shape[1]
    rows = MOBA_HEADS * t
    bps = n * PAGE_SIZE // MOBA_BLOCK
    lane = lax.broadcasted_iota(jnp.int32, (1, LANES), 1)
    lane_f = lane.astype(F32)

    def pv(p):
        p = p.astype(BF16)
        out = jnp.zeros(acc_scr.shape, F32)
        for pg in range(n):
            out += _dot_nt(p[:, pg * PAGE_SIZE:(pg + 1) * PAGE_SIZE], pages[pg][0])
        return out

    def step(st):
        if st == 0:
            gbps = n_full // g_ref.shape[1]
            g = jnp.zeros((rows, LANES), F32)
            for gs in range(g_ref.shape[1]):
                g = g + pltpu.roll(g_ref[0, gs], gs * gbps, 1)
            g = jnp.where(lane < n_full, g, -jnp.inf)
            sel = jnp.zeros((rows, LANES), F32)
            for _ in range(MOBA_TOPK):
                mx = jnp.max(g, axis=1, keepdims=True)
                first = jnp.min(jnp.where(g == mx, lane_f, float(LANES)), axis=1, keepdims=True)
                pick = lane_f == first
                sel = jnp.where(pick, 1.0, sel)
                g = jnp.where(pick, -jnp.inf, g)
            sel_scr[...] = sel
            m_scr[...] = jnp.full(m_scr.shape, NEG_BIG, F32)
            l_scr[...] = jnp.zeros_like(l_scr)
            acc_scr[...] = jnp.zeros_like(acc_scr)
        else:
            sel = sel_scr[...]
        picked = [jnp.broadcast_to(sel[:, st * bps + j:st * bps + j + 1], (rows, MOBA_BLOCK))
                  for j in range(bps)]
        pmask = jnp.concatenate(picked, axis=1) > 0.5
        _softmax_step(sc_ref[0].astype(F32), pmask, pv, m_scr, l_scr, acc_scr)
        if st == n_steps - 1:
            qbd = _block_diag_queries(q_ref[0]) * (MOBA_HEAD_DIM ** -0.5)
            row_t = lax.broadcasted_iota(jnp.int32, (rows, 1), 0) & (t - 1)
            s_own = _dot_nt(qbd, _pad_rows(kn_ref[0], LANES))
            v_own = _pad_rows(vn_ref[0], LANES)
            _softmax_step(s_own, lane <= row_t, lambda p: _dot(p, v_own), m_scr, l_scr, acc_scr)
            out = acc_scr[...] / l_scr[...]
            lane_w = lax.broadcasted_iota(jnp.int32, (1, D_MOBA), 1)
            o = jnp.zeros((t, D_MOBA), F32)
            for h in range(MOBA_HEADS):
                hm = (lane_w >= MOBA_HEAD_DIM * h) & (lane_w < MOBA_HEAD_DIM * (h + 1))
                o = o + jnp.where(hm, out[h * t:(h + 1) * t, :], 0.0)
            o_ref[0] = o

    for st in range(n_steps):
        @pl.when(s == st)
        def _(st=st):
            step(st)


def _moba_sample(qm, km, vm, scores, gates, v_pages, page_table):
    nb, t, _ = qm.shape
    n_pages = page_table.shape[1]
    past = n_pages * PAGE_SIZE
    n_full = past // MOBA_BLOCK
    assert past % MOBA_BLOCK == 0 and MOBA_TOPK <= n_full <= LANES and t & (t - 1) == 0
    n = PAGES_PER_STEP
    assert n_pages % n == 0 and (n * PAGE_SIZE) % MOBA_BLOCK == 0 and n_full % gates.shape[1] == 0
    steps = n_pages // n
    keys = n * PAGE_SIZE
    rows = MOBA_HEADS * t

    def page_spec(p):
        return pl.BlockSpec((1, D_MOBA, PAGE_SIZE), lambda b, s, pt, p=p: (pt[b, s * n + p], 0, 0))

    qspec = pl.BlockSpec((1, t, D_MOBA), lambda b, s, pt: (b, 0, 0))
    return pl.pallas_call(
        functools.partial(_moba_attend_kernel, n_full=n_full, n_steps=steps),
        out_shape=jax.ShapeDtypeStruct((nb, t, D_MOBA), F32),
        grid_spec=pltpu.PrefetchScalarGridSpec(
            num_scalar_prefetch=1, grid=(nb, steps),
            in_specs=[qspec,
                      pl.BlockSpec((1,) + gates.shape[1:], lambda b, s, pt: (b, 0, 0, 0)),
                      pl.BlockSpec((1, rows, keys), lambda b, s, pt: (b, 0, s)),
                      qspec, qspec] + [page_spec(p) for p in range(n)],
            out_specs=qspec,
            scratch_shapes=[pltpu.VMEM((rows, LANES), F32), pltpu.VMEM((rows, 1), F32),
                            pltpu.VMEM((rows, 1), F32), pltpu.VMEM((rows, D_MOBA), F32)]),
        compiler_params=_cparams(("parallel", "arbitrary")),
        name="moba_sample_attend",
    )(page_table, qm, gates, scores, km, vm, *([v_pages] * n))


def kernel(x_prompt, x_sample, cache_k, cache_v, state_gla, page_table, c_prompt, c_sample, w_ada, b_ada, ln_g, ln_b, w_ffn1_in, w_ffn1_out, w_mix_in, w_gk_up, b_gk, gla_norm_g, w_mix_out, w_ffn2_in, w_ffn2_out):
    depth = w_ada.shape[0]
    assert depth == 1, "one decoder layer"
    d = x_prompt.shape[-1]
    alpha = (2.0 * depth) ** 0.25
    nb_p, s_p, _ = x_prompt.shape
    nb_s, t_s, _ = x_sample.shape
    past_len = page_table.shape[1] * PAGE_SIZE

    w1_in, w1_out = w_ffn1_in[0].astype(BF16), w_ffn1_out[0].astype(BF16)
    w2_in, w2_out = w_ffn2_in[0].astype(BF16), w_ffn2_out[0].astype(BF16)
    w_mix = w_mix_in[0]
    w_main = w_mix[:, :D_MIX_MAIN].astype(BF16)
    w_rg = jnp.pad(w_mix[:, D_MIX_MAIN:], ((0, 0), (0, LANES - GLA_GATE_RANK))).astype(BF16)
    w_gk = jnp.pad(w_gk_up[0], ((0, LANES - GLA_GATE_RANK), (0, 0)))
    w_mo = w_mix_out[0].astype(BF16)

    c_all = jnp.concatenate([c_sample, c_prompt], axis=0)
    m = _ada_modulation(c_all, w_ada[0], b_ada[0])
    m4 = m.reshape(3 * N_SUBLAYERS, c_all.shape[0], 1, d)

    def first_half(x, m_row0, pos_base, kv_transposed):
        x = _ffn(x, m4, m_row0, 0, w1_in, w1_out, ln_g[0, 0], ln_b[0, 0], alpha)
        return x, _mixer_in(x, m4, m_row0, w_main, w_rg, w_gk, b_gk[0], pos_base, kv_transposed)

    def second_half(x, m_row0, o_moba, o_gla):
        return _ffn(x, m4, m_row0, 2, w2_in, w2_out, ln_g[0, 2], ln_b[0, 2], alpha,
                    mixer_out=(o_moba, o_gla, w_mo, ln_g[0, 1], ln_b[0, 1]))

    k_pages = jnp.transpose(cache_k[0], (0, 2, 3, 1)).reshape(cache_k.shape[1], D_MOBA, PAGE_SIZE)
    v_pages = jnp.transpose(cache_v[0], (0, 2, 3, 1)).reshape(cache_v.shape[1], D_MOBA, PAGE_SIZE)

    xs, (qm_s, ks, vs, qg_s, kg_s, vg_s, gg_s, la_s) = first_half(x_sample, 0, past_len, False)
    xp, (qm_p, kp_t, vp_t, qg_p, kg_p, vg_p, gg_p, la_p) = first_half(x_prompt, nb_s, 0, True)
    o_moba_p, scores, gates = _moba_prompt(qm_p, kp_t, vp_t, qm_s, k_pages, page_table)
    o_gla_p, sp = _gla_prompt(qg_p, kg_p, la_p, vg_p, gg_p, gla_norm_g[0])
    yp = second_half(xp, nb_s, o_moba_p, o_gla_p)
    o_moba_s = _moba_sample(qm_s, ks, vs, scores, gates, v_pages, page_table)
    o_gla_s, ss = _gla_sample(qg_s, kg_s, la_s, vg_s, gg_s, state_gla[0], gla_norm_g[0])
    ys = second_half(xs, 0, o_moba_s, o_gla_s)

    def rows_major(a_t):
        a = a_t.reshape(nb_p, MOBA_HEADS, MOBA_HEAD_DIM, s_p)
        return jnp.transpose(a, (0, 3, 1, 2))[None]

    def heads(a):
        return a.reshape(1, nb_s, t_s, MOBA_HEADS, MOBA_HEAD_DIM)

    return (yp, ys, rows_major(kp_t), rows_major(vp_t), sp, heads(ks), heads(vs), ss)
```

```python
import functools

import jax
import jax.numpy as jnp
from jax import lax
from jax.experimental import pallas as pl
from jax.experimental.pallas import tpu as pltpu

F32 = jnp.float32
BF16 = jnp.bfloat16

PAGE_SIZE = 128
MOBA_HEADS = 8
MOBA_HEAD_DIM = 64
D_MOBA = MOBA_HEADS * MOBA_HEAD_DIM
MOBA_BLOCK = 256
MOBA_TOPK = 3
ROPE_THETA = 500000.0
ROPE_DIMS = MOBA_HEAD_DIM // 4
ROPE_HALF = ROPE_DIMS // 2
GLA_HEADS = 4
GLA_DK = 64
GLA_DV = 128
D_GLA_K = GLA_HEADS * GLA_DK
D_GLA = GLA_HEADS * GLA_DV
GLA_GATE_RANK = 16
GLA_GATE_NORM = 16.0
D_MIX_MAIN = 3 * D_MOBA + 2 * D_GLA_K + 2 * D_GLA
N_SUBLAYERS = 3
LN_EPS = 1e-5
RMS_EPS = 1e-6

LANES = 128
NEG_BIG = -1e30
VMEM_LIMIT = 56 * 1024 * 1024

FFN_ROWS = 512
FFN_CHUNKS = 1
GLA_CHUNK = 128
GLA_LEAF = 32
PAGES_PER_STEP = 64


def _cparams(sem):
    return pltpu.CompilerParams(dimension_semantics=sem, vmem_limit_bytes=VMEM_LIMIT)


def _dot(a, b):
    return jnp.dot(a.astype(BF16), b.astype(BF16), preferred_element_type=F32)


def _dot_nt(a, b):
    return lax.dot_general(a.astype(BF16), b.astype(BF16), (((1,), (1,)), ((), ())),
                           preferred_element_type=F32)


def _split2(x):
    hi = x.astype(BF16)
    lo = (x - hi.astype(F32)).astype(BF16)
    return hi, lo


def _dot3(a, b, nt=False):
    d = _dot_nt if nt else _dot
    ah, al = _split2(a)
    bh, bl = _split2(b)
    return d(ah, bh) + (d(ah, bl) + d(al, bh))


def _dot_exact_lhs(lhs_bf16, x):
    hi = x.astype(BF16)
    r1 = x - hi.astype(F32)
    mid = r1.astype(BF16)
    lo = (r1 - mid.astype(F32)).astype(BF16)
    f = functools.partial(jnp.dot, lhs_bf16, preferred_element_type=F32)
    return f(hi) + (f(mid) + f(lo))


def _silu(x):
    return x * jax.nn.sigmoid(x)


def _layer_norm(y, g, b):
    mu = jnp.mean(y, axis=-1, keepdims=True)
    yc = y - mu
    var = jnp.mean(yc * yc, axis=-1, keepdims=True)
    return yc * lax.rsqrt(var + LN_EPS) * g + b


def _pad_rows(x, rows):
    return jnp.concatenate([x, jnp.zeros((rows - x.shape[0], x.shape[1]), x.dtype)], axis=0)


def _ada_kernel(c_ref, w_ref, b_ref, o_ref):
    o_ref[0] = _dot3(_silu(c_ref[...]), w_ref[...]) + b_ref[...]


def _ada_modulation(c_all, w_ada, b_ada):
    nb, d = c_all.shape
    n_out = w_ada.shape[1] // d
    return pl.pallas_call(
        _ada_kernel,
        out_shape=jax.ShapeDtypeStruct((n_out, nb, d), F32),
        grid=(n_out,),
        in_specs=[pl.BlockSpec((nb, d), lambda n: (0, 0)),
                  pl.BlockSpec((d, d), lambda n: (0, n)),
                  pl.BlockSpec((1, d), lambda n: (0, n))],
        out_specs=pl.BlockSpec((1, nb, d), lambda n: (n, 0, 0)),
        compiler_params=_cparams(("arbitrary",)),
        name="ada_modulation",
    )(c_all, w_ada, b_ada.reshape(1, -1))


def _ffn_kernel(*refs, alpha, fused_mixer_out, n_chunks):
    if fused_mixer_out:
        x_ref, om_ref, og_ref, mm_ref, wmo_ref, gm_ref, bm_ref = refs[:7]
        m_ref, wa_ref, wu_ref, wo_ref, g_ref, b_ref, o_ref, h_scr, acc_scr, res_scr = refs[7:]
    else:
        x_ref, m_ref, wa_ref, wu_ref, wo_ref, g_ref, b_ref, o_ref, h_scr, acc_scr = refs
        res_scr = x_ref
    j = pl.program_id(1)
    bb, ts, d = x_ref.shape
    tm = bb * ts

    def modulated_input():
        x = x_ref[...]
        if fused_mixer_out:
            mix = (_dot(om_ref[...].reshape(tm, D_MOBA), wmo_ref[0:D_MOBA, :])
                   + _dot(og_ref[...].reshape(tm, D_GLA), wmo_ref[D_MOBA:D_MOBA + D_GLA, :]))
            x = _layer_norm(alpha * x + (1.0 + mm_ref[2]) * mix.reshape(bb, ts, d), gm_ref[...], bm_ref[...])
            res_scr[...] = x
        h = (x * (1.0 + m_ref[1]) + m_ref[0]).reshape(tm, d).astype(BF16)
        h_scr[...] = h
        return h

    def chunk(h):
        a = jnp.dot(h, wa_ref[...], preferred_element_type=F32)
        u = jnp.dot(h, wu_ref[...], preferred_element_type=F32)
        t = (_silu(a) * u).astype(BF16)
        return jnp.dot(t, wo_ref[...], preferred_element_type=F32)

    def post_norm(acc):
        y = alpha * res_scr[...] + (0.5 * (1.0 + m_ref[2])) * acc.reshape(bb, ts, d)
        o_ref[...] = _layer_norm(y, g_ref[...], b_ref[...])

    @pl.when(j == 0)
    def _():
        acc = chunk(modulated_input())
        if n_chunks == 1:
            post_norm(acc)
        else:
            acc_scr[...] = acc

    if n_chunks > 2:
        @pl.when((j > 0) & (j < n_chunks - 1))
        def _():
            acc_scr[...] += chunk(h_scr[...])

    if n_chunks > 1:
        @pl.when(j == n_chunks - 1)
        def _():
            post_norm(acc_scr[...] + chunk(h_scr[...]))


def _row_tiling(x):
    nb, s, _ = x.shape
    if s >= FFN_ROWS:
        assert s % FFN_ROWS == 0
        return 1, FFN_ROWS
    assert s % 8 == 0 and FFN_ROWS % s == 0
    bb = min(nb, FFN_ROWS // s)
    assert nb % bb == 0
    return bb, s


def _ffn(x, m4, m_row0, sub, w_in, w_out, ln_g, ln_b, alpha, mixer_out=None, n_chunks=FFN_CHUNKS):
    nb, s, d = x.shape
    bb, ts = _row_tiling(x)
    tpb = s // ts
    d_ff = w_out.shape[0]
    ck = d_ff // n_chunks
    assert ck * n_chunks == d_ff and ck % LANES == 0 and m_row0 % bb == 0
    grid = ((nb // bb) * tpb, n_chunks)
    xmap = lambda i, j: (i // tpb, i % tpb, 0)
    const2 = lambda i, j: (0, 0)

    def mspec(k):
        return pl.BlockSpec((3, bb, 1, d), lambda i, j: (k, m_row0 // bb + i // tpb, 0, 0))

    vec = pl.BlockSpec((1, d), const2)
    in_specs = [pl.BlockSpec((bb, ts, d), xmap)]
    args = [x]
    scratch = [pltpu.VMEM((bb * ts, d), BF16), pltpu.VMEM((bb * ts, d), F32)]
    if mixer_out is not None:
        o_moba, o_gla, w_mo, g_mo, b_mo = mixer_out
        in_specs += [pl.BlockSpec((bb, ts, D_MOBA), xmap), pl.BlockSpec((bb, ts, D_GLA), xmap),
                     mspec(1), pl.BlockSpec(w_mo.shape, const2), vec, vec]
        args += [o_moba, o_gla, m4, w_mo, g_mo.reshape(1, d), b_mo.reshape(1, d)]
        scratch.append(pltpu.VMEM((bb, ts, d), F32))
    in_specs += [mspec(sub),
                 pl.BlockSpec((d, ck), lambda i, j: (0, j)),
                 pl.BlockSpec((d, ck), lambda i, j: (0, j + n_chunks)),
                 pl.BlockSpec((ck, d), lambda i, j: (j, 0)),
                 vec, vec]
    args += [m4, w_in, w_in, w_out, ln_g.reshape(1, d), ln_b.reshape(1, d)]
    return pl.pallas_call(
        functools.partial(_ffn_kernel, alpha=alpha, fused_mixer_out=mixer_out is not None,
                          n_chunks=n_chunks),
        out_shape=jax.ShapeDtypeStruct(x.shape, F32),
        grid=grid,
        in_specs=in_specs,
        out_specs=pl.BlockSpec((bb, ts, d), xmap),
        scratch_shapes=scratch,
        compiler_params=_cparams(("parallel", "arbitrary")),
        name="ffn_postnorm",
    )(*args)


def _rope_table_kernel(o_ref, *, pos_base):
    tt = o_ref.shape[1]
    lane = lax.broadcasted_iota(jnp.int32, (1, LANES), 1)
    fi = (lane & (ROPE_HALF - 1)).astype(F32)
    inv = jnp.power(jnp.full((1, LANES), ROPE_THETA, F32), -fi / ROPE_HALF)
    row = lax.broadcasted_iota(jnp.int32, (tt, 1), 0)
    pos = (pos_base + pl.program_id(0) * tt + row).astype(F32)
    ang = pos * inv
    cos = jnp.cos(ang)
    sin = jnp.sin(ang)
    l64 = lane & (MOBA_HEAD_DIM - 1)
    o_ref[0] = jnp.where(l64 < ROPE_DIMS, cos, 1.0)
    o_ref[1] = jnp.where(l64 < ROPE_HALF, -sin, 0.0)
    o_ref[2] = jnp.where((l64 >= ROPE_HALF) & (l64 < ROPE_DIMS), sin, 0.0)


def _rope_tables(n_pos, tile, pos_base):
    return pl.pallas_call(
        functools.partial(_rope_table_kernel, pos_base=pos_base),
        out_shape=jax.ShapeDtypeStruct((3, n_pos, LANES), F32),
        grid=(n_pos // tile,),
        in_specs=[],
        out_specs=pl.BlockSpec((3, tile, LANES), lambda i: (0, i, 0)),
        compiler_params=_cparams(("arbitrary",)),
        name="rope_tables",
    )()


def _mixin_kernel(x_ref, m_ref, tab_ref, w_ref, wrg_ref, wgk_ref, bgk_ref,
                  qm_ref, km_ref, vm_ref, qg_ref, kg_ref, vg_ref, gg_ref, la_ref,
                  *, kv_transposed):
    bb, ts, d = x_ref.shape
    tm = bb * ts
    h = (x_ref[...] * (1.0 + m_ref[1]) + m_ref[0]).reshape(tm, d).astype(BF16)
    p = jnp.dot(h, w_ref[...], preferred_element_type=F32)
    c_tab, s_lo, s_hi = tab_ref[0], tab_ref[1], tab_ref[2]

    def rope(x):
        slabs = []
        for s in range(x.shape[1] // LANES):
            xs = x[:, s * LANES:(s + 1) * LANES]
            r = (xs.reshape(bb, ts, LANES) * c_tab
                 + pltpu.roll(xs, LANES - ROPE_HALF, 1).reshape(bb, ts, LANES) * s_lo
                 + pltpu.roll(xs, ROPE_HALF, 1).reshape(bb, ts, LANES) * s_hi)
            slabs.append(r.reshape(tm, LANES))
        return jnp.concatenate(slabs, axis=1)

    def store(ref, x):
        ref[...] = x.reshape(ref.shape)

    def store_kv(ref, x):
        if kv_transposed:
            ref[0] = x.T
        else:
            store(ref, x)

    o = 0
    store(qm_ref, rope(p[:, o:o + D_MOBA])); o += D_MOBA
    store_kv(km_ref, rope(p[:, o:o + D_MOBA])); o += D_MOBA
    store_kv(vm_ref, p[:, o:o + D_MOBA]); o += D_MOBA
    store(qg_ref, p[:, o:o + D_GLA_K] * (GLA_DK ** -0.5)); o += D_GLA_K
    store(kg_ref, p[:, o:o + D_GLA_K]); o += D_GLA_K
    store(vg_ref, p[:, o:o + D_GLA]); o += D_GLA
    store(gg_ref, p[:, o:o + D_GLA]); o += D_GLA

    rg = jnp.dot(h, wrg_ref[...], preferred_element_type=F32)
    z = _dot3(rg, wgk_ref[...]) + bgk_ref[...]
    log_sig = jnp.minimum(z, 0.0) - jnp.log1p(jnp.exp(-jnp.abs(z)))
    store(la_ref, log_sig / GLA_GATE_NORM)


def _mixer_in(x, m4, m_row0, w_main, w_rg, w_gk, b_gk, pos_base, kv_transposed):
    nb, s, d = x.shape
    bb, ts = _row_tiling(x)
    tpb = s // ts
    assert m_row0 % bb == 0 and (bb == 1 or not kv_transposed)
    grid = ((nb // bb) * tpb,)
    xmap = lambda i: (i // tpb, i % tpb, 0)
    tmap = lambda i: (i // tpb, 0, i % tpb)
    mmap = lambda i: (1, m_row0 // bb + i // tpb, 0, 0)
    const2 = lambda i: (0, 0)
    widths = (D_MOBA, D_MOBA, D_MOBA, D_GLA_K, D_GLA_K, D_GLA, D_GLA, D_GLA_K)
    out_shape = [jax.ShapeDtypeStruct((nb, s, w), F32) for w in widths]
    out_specs = [pl.BlockSpec((bb, ts, w), xmap) for w in widths]
    if kv_transposed:
        for n in (1, 2):
            out_shape[n] = jax.ShapeDtypeStruct((nb, D_MOBA, s), F32)
            out_specs[n] = pl.BlockSpec((1, D_MOBA, ts), tmap)
    tables = _rope_tables(s, ts, pos_base)
    return pl.pallas_call(
        functools.partial(_mixin_kernel, kv_transposed=kv_transposed),
        out_shape=out_shape,
        grid=grid,
        in_specs=[pl.BlockSpec((bb, ts, d), xmap),
                  pl.BlockSpec((3, bb, 1, d), mmap),
                  pl.BlockSpec((3, ts, LANES), lambda i: (0, i % tpb, 0)),
                  pl.BlockSpec(w_main.shape, const2),
                  pl.BlockSpec(w_rg.shape, const2),
                  pl.BlockSpec(w_gk.shape, const2),
                  pl.BlockSpec((1, D_GLA_K), const2)],
        out_specs=out_specs,
        compiler_params=_cparams(("parallel",)),
        name="mixer_in",
    )(x, m4, tables, w_main, w_rg, w_gk, b_gk.reshape(1, -1))


def _softmax_step(s, pmask, pv, m_scr, l_scr, acc_scr):
    m_old = m_scr[...]
    m_new = jnp.maximum(m_old, jnp.max(jnp.where(pmask, s, NEG_BIG), axis=1, keepdims=True))
    p = jnp.where(pmask, jnp.exp(s - m_new), 0.0)
    alpha = jnp.exp(m_old - m_new)
    l_scr[...] = alpha * l_scr[...] + jnp.sum(p, axis=1, keepdims=True)
    acc_scr[...] = alpha * acc_scr[...] + pv(p)
    m_scr[...] = m_new


def _moba_prompt_kernel(pt_ref, q_ref, qn_ref, kt_ref, vt_ref, qs_ref, *refs, pages_per_step, lead):
    del pt_ref
    pages = refs[:pages_per_step]
    o_ref, sc_ref, g_ref, kmean_scr, sel_scr = refs[pages_per_step:]
    i = pl.program_id(2)
    blk = q_ref.shape[1]
    s_len = kt_ref.shape[2]
    n_blocks = s_len // blk
    nb8 = kmean_scr.shape[1]
    hd = MOBA_HEAD_DIM
    blk_shift = blk.bit_length() - 1
    blk_row = lax.broadcasted_iota(jnp.int32, (nb8, 1), 0)

    @pl.when(i == 0)
    def _():
        blk_of_key = lax.shift_right_logical(lax.broadcasted_iota(jnp.int32, (1, s_len), 1), blk_shift)
        pool = jnp.where(blk_row == blk_of_key, 1.0 / blk, 0.0).astype(BF16)
        for h in range(2):
            kth = kt_ref[0, h * hd:(h + 1) * hd, :]
            hi = kth.astype(BF16)
            r1 = kth - hi.astype(F32)
            mid = r1.astype(BF16)
            lo = (r1 - mid.astype(F32)).astype(BF16)
            kmean_scr[h] = _dot_nt(pool, hi) + (_dot_nt(pool, mid) + _dot_nt(pool, lo))

    row = lax.broadcasted_iota(jnp.int32, (blk, 1), 0)
    col = lax.broadcasted_iota(jnp.int32, (1, blk), 1)
    causal = col <= row

    def picked_blocks(qh, h, own):
        gt = jnp.where(blk_row < own, _dot3(kmean_scr[h], qh, nt=True), -jnp.inf)
        rank = jnp.zeros(gt.shape, F32)
        for m in range(own):
            gm = gt[m:m + 1, :]
            beats = (gm > gt) | ((gm == gt) & (m < blk_row))
            rank += jnp.where(beats, 1.0, 0.0)
        sel_t = jnp.where((blk_row < own) & (rank < MOBA_TOPK), 1.0, 0.0)
        return _pad_rows(sel_t, LANES).T

    def attend(own):
        q = q_ref[0]
        n = (own + 1) * blk
        outs = []
        for h in range(2):
            qh = q[:, h * hd:(h + 1) * hd]
            s = _dot((qh * (hd ** -0.5)).astype(BF16), kt_ref[0, h * hd:(h + 1) * hd, 0:n])
            if own <= MOBA_TOPK:
                parts = [s[:, 0:own * blk]] if own else []
            else:
                sel = sel_scr[own - lead, h]
                parts = []
                for j in range(own):
                    picked = jnp.broadcast_to(sel[:, j:j + 1], (blk, blk)) > 0.5
                    parts.append(jnp.where(picked, s[:, j * blk:(j + 1) * blk], -jnp.inf))
            parts.append(jnp.where(causal, s[:, own * blk:n], -jnp.inf))
            sm = jnp.concatenate(parts, axis=1)
            p = jnp.exp(sm - jnp.max(sm, axis=1, keepdims=True))
            l = jnp.sum(p, axis=1, keepdims=True)
            outs.append(_dot_nt(p, vt_ref[0, h * hd:(h + 1) * hd, 0:n]) / l)
        return jnp.concatenate(outs, axis=1)

    def score_pages():
        qbd = (_block_diag_queries(qs_ref[0]) * (hd ** -0.5)).astype(BF16)
        lane = lax.broadcasted_iota(jnp.int32, (1, LANES), 1)
        ppb = MOBA_BLOCK // PAGE_SIZE
        g = jnp.zeros((qbd.shape[0], LANES), F32)
        for p in range(pages_per_step):
            s = _dot(qbd, pages[p][0])
            sc_ref[0, :, p * PAGE_SIZE:(p + 1) * PAGE_SIZE] = s.astype(sc_ref.dtype)
            psum = jnp.sum(s, axis=1, keepdims=True)
            bs = psum if p % ppb == 0 else bs + psum
            if p % ppb == ppb - 1:
                g = jnp.where(lane == p // ppb, bs, g)
        g_ref[0, 0] = g

    for own in range(n_blocks):
        @pl.when(i == own)
        def _(own=own):
            o_ref[0] = attend(own)
            if MOBA_TOPK < own + lead < n_blocks:
                qn = qn_ref[0]
                for h in range(2):
                    sel_scr[own, h] = picked_blocks(qn[:, h * hd:(h + 1) * hd], h, own + lead)
            score_pages()


def _moba_prompt(qm, kt, vt, q_new, k_pages, page_table):
    nb, s, _ = qm.shape
    blk = MOBA_BLOCK
    assert s % blk == 0 and s // blk <= LANES
    n_pairs = D_MOBA // LANES
    n_tiles = s // blk
    nb8 = -(-n_tiles // 8) * 8
    nb_new, t, _ = q_new.shape
    n_pages = page_table.shape[1]
    pps = n_pages // n_tiles
    assert nb_new == nb * n_pairs and pps * n_tiles == n_pages and (pps * PAGE_SIZE) % MOBA_BLOCK == 0
    rows = MOBA_HEADS * t
    keys = pps * PAGE_SIZE
    lead = MOBA_TOPK + 1
    qmap = lambda b, hp, i, pt: (b, i, hp)
    qnmap = lambda b, hp, i, pt: (b, jnp.minimum(i + lead, n_tiles - 1), hp)
    kmap = lambda b, hp, i, pt: (b, hp, 0)

    def page_spec(p):
        return pl.BlockSpec((1, D_MOBA, PAGE_SIZE),
                            lambda b, hp, i, pt, p=p: (pt[b * n_pairs + hp, i * pps + p], 0, 0))

    return pl.pallas_call(
        functools.partial(_moba_prompt_kernel, pages_per_step=pps, lead=lead),
        out_shape=[jax.ShapeDtypeStruct((nb, s, D_MOBA), F32),
                   jax.ShapeDtypeStruct((nb_new, rows, n_pages * PAGE_SIZE), BF16),
                   jax.ShapeDtypeStruct((nb_new, n_tiles, rows, LANES), F32)],
        grid_spec=pltpu.PrefetchScalarGridSpec(
            num_scalar_prefetch=1, grid=(nb, n_pairs, n_tiles),
            in_specs=[pl.BlockSpec((1, blk, LANES), qmap),
                      pl.BlockSpec((1, blk, LANES), qnmap),
                      pl.BlockSpec((1, LANES, s), kmap),
                      pl.BlockSpec((1, LANES, s), kmap),
                      pl.BlockSpec((1, t, D_MOBA), lambda b, hp, i, pt: (b * n_pairs + hp, 0, 0))]
                     + [page_spec(p) for p in range(pps)],
            out_specs=[pl.BlockSpec((1, blk, LANES), qmap),
                       pl.BlockSpec((1, rows, keys), lambda b, hp, i, pt: (b * n_pairs + hp, 0, i)),
                       pl.BlockSpec((1, 1, rows, LANES), lambda b, hp, i, pt: (b * n_pairs + hp, i, 0, 0))],
            scratch_shapes=[pltpu.VMEM((2, nb8, MOBA_HEAD_DIM), F32),
                            pltpu.VMEM((max(n_tiles - lead, 1), 2, blk, LANES), F32)]),
        compiler_params=_cparams(("parallel", "parallel", "arbitrary")),
        name="moba_prompt",
    )(page_table, qm, qm, kt, vt, q_new, *([k_pages] * pps))


def _gla_out(o, gg, ng):
    ms = jnp.mean(o * o, axis=1, keepdims=True)
    return o * lax.rsqrt(ms + RMS_EPS) * ng * _silu(gg)


def _gla_prompt_kernel(q_ref, k_ref, la_ref, v_ref, gg_ref, ng_ref, o_ref, s_ref, st_scr):
    c = GLA_CHUNK
    n_chunks = q_ref.shape[1] // c
    lane = lax.broadcasted_iota(jnp.int32, (1, LANES), 1)
    row = lax.broadcasted_iota(jnp.int32, (c, 1), 0)
    col = lax.broadcasted_iota(jnp.int32, (1, c), 1)
    tril_bf = jnp.where(col <= row, 1.0, 0.0).astype(BF16)
    ng = ng_ref[...]
    st_scr[...] = jnp.zeros_like(st_scr)

    halves = []
    half = c // 2
    while half >= GLA_LEAF:
        halves.append(half)
        half //= 2
    def same_block(size):
        sh = size.bit_length() - 1
        return lax.shift_right_logical(row, sh) == lax.shift_right_logical(col, sh)

    pair_masks = [same_block(2 * hf) & ((row & hf) != 0) & ((col & hf) == 0) for hf in halves]
    leaf_mask = same_block(GLA_LEAF) & (col <= row)

    def rows_of(b, size, pick):
        parts = []
        for n in range(c // size):
            r = pick(n)
            src = b[r:r + 1, :] if r >= 0 else jnp.zeros((1, LANES), F32)
            parts.append(jnp.broadcast_to(src, (size, LANES)))
        return jnp.concatenate(parts, axis=0)

    def body(ci, carry):
        r0 = pl.multiple_of(ci * c, c)
        for hp in range(GLA_HEADS // 2):
            sl = slice(hp * LANES, (hp + 1) * LANES)
            q = q_ref[0, pl.ds(r0, c), sl]
            k = k_ref[0, pl.ds(r0, c), sl]
            b = _dot_exact_lhs(tril_bf, la_ref[0, pl.ds(r0, c), sl])
            b_end = b[c - 1:c, :]
            q_dec = q * jnp.exp(b)
            k_dec = k * jnp.exp(b_end - b)
            e_end = jnp.exp(b_end)
            level_qk = []
            for hf in halves:
                u = jnp.exp(-jnp.abs(b - rows_of(b, 2 * hf, lambda n: n * 2 * hf + hf - 1)))
                level_qk.append((q * u, k * u))
            b_leaf = b - rows_of(b, GLA_LEAF, lambda n: n * GLA_LEAF - 1)
            q_leaf, k_leaf = q * jnp.exp(b_leaf), k * jnp.exp(-b_leaf)
            for hh in range(2):
                h = 2 * hp + hh
                hm = (lane >= GLA_DK * hh) & (lane < GLA_DK * (hh + 1))
                qb = jnp.where(hm, q_dec, 0.0)
                vh = v_ref[0, pl.ds(r0, c), h * GLA_DV:(h + 1) * GLA_DV]
                att = jnp.where(leaf_mask, _dot_nt(jnp.where(hm, q_leaf, 0.0), k_leaf), 0.0)
                for mask, (ql, kl) in zip(pair_masks, level_qk):
                    att = jnp.where(mask, _dot_nt(jnp.where(hm, ql, 0.0), kl), att)
                st = st_scr[h]
                o = _dot(att, vh) + _dot_nt(qb, st)
                st_scr[h] = st * e_end + _dot(vh.T, jnp.where(hm, k_dec, 0.0))
                gg = gg_ref[0, pl.ds(r0, c), h * GLA_DV:(h + 1) * GLA_DV]
                o_ref[0, pl.ds(r0, c), h * GLA_DV:(h + 1) * GLA_DV] = _gla_out(o, gg, ng)
        return carry

    lax.fori_loop(0, n_chunks, body, 0, unroll=4)
    for h in range(GLA_HEADS):
        hh = h % 2
        s_ref[0, 0, h] = st_scr[h].T[hh * GLA_DK:(hh + 1) * GLA_DK, :]


def _gla_prompt(qg, kg, la, vg, gg, norm_g):
    nb, s, _ = qg.shape
    assert s % GLA_CHUNK == 0
    map3 = lambda b: (b, 0, 0)
    return pl.pallas_call(
        _gla_prompt_kernel,
        out_shape=[jax.ShapeDtypeStruct((nb, s, D_GLA), F32),
                   jax.ShapeDtypeStruct((1, nb, GLA_HEADS, GLA_DK, GLA_DV), F32)],
        grid=(nb,),
        in_specs=[pl.BlockSpec((1, s, D_GLA_K), map3),
                  pl.BlockSpec((1, s, D_GLA_K), map3),
                  pl.BlockSpec((1, s, D_GLA_K), map3),
                  pl.BlockSpec((1, s, D_GLA), map3),
                  pl.BlockSpec((1, s, D_GLA), map3),
                  pl.BlockSpec((1, GLA_DV), lambda b: (0, 0))],
        out_specs=[pl.BlockSpec((1, s, D_GLA), map3),
                   pl.BlockSpec((1, 1, GLA_HEADS, GLA_DK, GLA_DV), lambda b: (0, b, 0, 0, 0))],
        scratch_shapes=[pltpu.VMEM((GLA_HEADS, GLA_DV, LANES), F32)],
        compiler_params=_cparams(("parallel",)),
        name="gla_prompt",
    )(qg, kg, la, vg, gg, norm_g.reshape(1, GLA_DV))


def _gla_sample_kernel(q_ref, k_ref, la_ref, v_ref, gg_ref, s0_ref, ng_ref, o_ref, s_ref):
    rows_per_step, t = q_ref.shape[0], q_ref.shape[1]
    lane = lax.broadcasted_iota(jnp.int32, (1, LANES), 1)
    row = lax.broadcasted_iota(jnp.int32, (t, 1), 0)
    ng = ng_ref[...]
    eye = (lax.broadcasted_iota(jnp.int32, (GLA_DK, 1), 0)
           == lax.broadcasted_iota(jnp.int32, (1, GLA_DK), 1))
    zeros_half = jnp.zeros((GLA_DK, GLA_DV), F32)
    for bi in range(rows_per_step):
        for hp in range(GLA_HEADS // 2):
            sl = slice(hp * LANES, (hp + 1) * LANES)
            q = q_ref[bi, :, sl]
            k = k_ref[bi, :, sl]
            b = la_ref[bi, :, sl]
            sh = 1
            while sh < t:
                b = b + jnp.where(row >= sh, pltpu.roll(b, sh, 0), 0.0)
                sh *= 2
            b_end = b[t - 1:t, :]
            e_end = jnp.exp(b_end)
            q_dec = q * jnp.exp(b)
            k_inv = _pad_rows(k * jnp.exp(-b), LANES)
            k_dec = k * jnp.exp(b_end - b)
            for hh in range(2):
                h = 2 * hp + hh
                hm = (lane >= GLA_DK * hh) & (lane < GLA_DK * (hh + 1))
                qb = jnp.where(hm, q_dec, 0.0)
                vh = _pad_rows(v_ref[bi, :, h * GLA_DV:(h + 1) * GLA_DV], LANES)
                s0 = s0_ref[bi, h]
                s0_pad = jnp.concatenate([s0, zeros_half] if hh == 0 else [zeros_half, s0], axis=0)
                att = jnp.where(lane <= row, _dot_nt(qb, k_inv), 0.0)
                o = _dot(att, vh) + _dot(qb, s0_pad)
                gg = gg_ref[bi, :, h * GLA_DV:(h + 1) * GLA_DV]
                o_ref[bi, :, h * GLA_DV:(h + 1) * GLA_DV] = _gla_out(o, gg, ng)
                kd = _pad_rows(jnp.where(hm, k_dec, 0.0), LANES)
                upd = _dot(kd.T, vh)[hh * GLA_DK:(hh + 1) * GLA_DK, :]
                e_h = e_end[:, hh * GLA_DK:(hh + 1) * GLA_DK]
                diag = jnp.where(eye, jnp.broadcast_to(e_h, (GLA_DK, GLA_DK)), 0.0)
                s_ref[0, bi, h] = _dot3(diag, s0) + upd


def _gla_sample(qg, kg, la, vg, gg, s0, norm_g, rows_per_step=4):
    nb, t, _ = qg.shape
    r = rows_per_step
    assert nb % r == 0
    map3 = lambda b: (b, 0, 0)
    return pl.pallas_call(
        _gla_sample_kernel,
        out_shape=[jax.ShapeDtypeStruct((nb, t, D_GLA), F32),
                   jax.ShapeDtypeStruct((1, nb, GLA_HEADS, GLA_DK, GLA_DV), F32)],
        grid=(nb // r,),
        in_specs=[pl.BlockSpec((r, t, D_GLA_K), map3),
                  pl.BlockSpec((r, t, D_GLA_K), map3),
                  pl.BlockSpec((r, t, D_GLA_K), map3),
                  pl.BlockSpec((r, t, D_GLA), map3),
                  pl.BlockSpec((r, t, D_GLA), map3),
                  pl.BlockSpec((r, GLA_HEADS, GLA_DK, GLA_DV), lambda b: (b, 0, 0, 0)),
                  pl.BlockSpec((1, GLA_DV), lambda b: (0, 0))],
        out_specs=[pl.BlockSpec((r, t, D_GLA), map3),
                   pl.BlockSpec((1, r, GLA_HEADS, GLA_DK, GLA_DV), lambda b: (0, b, 0, 0, 0))],
        compiler_params=_cparams(("parallel",)),
        name="gla_sample",
    )(qg, kg, la, vg, gg, s0, norm_g.reshape(1, GLA_DV))


def _block_diag_queries(q):
    lane = lax.broadcasted_iota(jnp.int32, (1, D_MOBA), 1)
    parts = [jnp.where((lane >= MOBA_HEAD_DIM * h) & (lane < MOBA_HEAD_DIM * (h + 1)), q, 0.0)
             for h in range(MOBA_HEADS)]
    return jnp.concatenate(parts, axis=0)


def _moba_attend_kernel(pt_ref, q_ref, g_ref, sc_ref, kn_ref, vn_ref, *refs, n_full, n_steps):
    del pt_ref
    n = PAGES_PER_STEP
    pages, o_ref = refs[:n], refs[n]
    sel_scr, m_scr, l_scr, acc_scr = refs[n + 1:]
    s = pl.program_id(1)
    t = q_ref.shape[1]
    rows = MOBA_HEADS * t
    bps = n * PAGE_SIZE // MOBA_BLOCK
    lane = lax.broadcasted_iota(jnp.int32, (1, LANES), 1)
    lane_f = lane.astype(F32)

    def pv(p):
        p = p.astype(BF16)
        out = jnp.zeros(acc_scr.shape, F32)
        for pg in range(n):
            out += _dot_nt(p[:, pg * PAGE_SIZE:(pg + 1) * PAGE_SIZE], pages[pg][0])
        return out

    def step(st):
        if st == 0:
            gbps = n_full // g_ref.shape[1]
            g = jnp.zeros((rows, LANES), F32)
            for gs in range(g_ref.shape[1]):
                g = g + pltpu.roll(g_ref[0, gs], gs * gbps, 1)
            g = jnp.where(lane < n_full, g, -jnp.inf)
            sel = jnp.zeros((rows, LANES), F32)
            for _ in range(MOBA_TOPK):
                mx = jnp.max(g, axis=1, keepdims=True)
                first = jnp.min(jnp.where(g == mx, lane_f, float(LANES)), axis=1, keepdims=True)
                pick = lane_f == first
                sel = jnp.where(pick, 1.0, sel)
                g = jnp.where(pick, -jnp.inf, g)
            sel_scr[...] = sel
            m_scr[...] = jnp.full(m_scr.shape, NEG_BIG, F32)
            l_scr[...] = jnp.zeros_like(l_scr)
            acc_scr[...] = jnp.zeros_like(acc_scr)
        else:
            sel = sel_scr[...]
        picked = [jnp.broadcast_to(sel[:, st * bps + j:st * bps + j + 1], (rows, MOBA_BLOCK))
                  for j in range(bps)]
        pmask = jnp.concatenate(picked, axis=1) > 0.5
        _softmax_step(sc_ref[0].astype(F32), pmask, pv, m_scr, l_scr, acc_scr)
        if st == n_steps - 1:
            qbd = _block_diag_queries(q_ref[0]) * (MOBA_HEAD_DIM ** -0.5)
            row_t = lax.broadcasted_iota(jnp.int32, (rows, 1), 0) & (t - 1)
            s_own = _dot_nt(qbd, _pad_rows(kn_ref[0], LANES))
            v_own = _pad_rows(vn_ref[0], LANES)
            _softmax_step(s_own, lane <= row_t, lambda p: _dot(p, v_own), m_scr, l_scr, acc_scr)
            out = acc_scr[...] / l_scr[...]
            lane_w = lax.broadcasted_iota(jnp.int32, (1, D_MOBA), 1)
            o = jnp.zeros((t, D_MOBA), F32)
            for h in range(MOBA_HEADS):
                hm = (lane_w >= MOBA_HEAD_DIM * h) & (lane_w < MOBA_HEAD_DIM * (h + 1))
                o = o + jnp.where(hm, out[h * t:(h + 1) * t, :], 0.0)
            o_ref[0] = o

    for st in range(n_steps):
        @pl.when(s == st)
        def _(st=st):
            step(st)


def _moba_sample(qm, km, vm, scores, gates, v_pages, page_table):
    nb, t, _ = qm.shape
    n_pages = page_table.shape[1]
    past = n_pages * PAGE_SIZE
    n_full = past // MOBA_BLOCK
    assert past % MOBA_BLOCK == 0 and MOBA_TOPK <= n_full <= LANES and t & (t - 1) == 0
    n = PAGES_PER_STEP
    assert n_pages % n == 0 and (n * PAGE_SIZE) % MOBA_BLOCK == 0 and n_full % gates.shape[1] == 0
    steps = n_pages // n
    keys = n * PAGE_SIZE
    rows = MOBA_HEADS * t

    def page_spec(p):
        return pl.BlockSpec((1, D_MOBA, PAGE_SIZE), lambda b, s, pt, p=p: (pt[b, s * n + p], 0, 0))

    qspec = pl.BlockSpec((1, t, D_MOBA), lambda b, s, pt: (b, 0, 0))
    return pl.pallas_call(
        functools.partial(_moba_attend_kernel, n_full=n_full, n_steps=steps),
        out_shape=jax.ShapeDtypeStruct((nb, t, D_MOBA), F32),
        grid_spec=pltpu.PrefetchScalarGridSpec(
            num_scalar_prefetch=1, grid=(nb, steps),
            in_specs=[qspec,
                      pl.BlockSpec((1,) + gates.shape[1:], lambda b, s, pt: (b, 0, 0, 0)),
                      pl.BlockSpec((1, rows, keys), lambda b, s, pt: (b, 0, s)),
                      qspec, qspec] + [page_spec(p) for p in range(n)],
            out_specs=qspec,
            scratch_shapes=[pltpu.VMEM((rows, LANES), F32), pltpu.VMEM((rows, 1), F32),
                            pltpu.VMEM((rows, 1), F32), pltpu.VMEM((rows, D_MOBA), F32)]),
        compiler_params=_cparams(("parallel", "arbitrary")),
        name="moba_sample_attend",
    )(page_table, qm, gates, scores, km, vm, *([v_pages] * n))


def kernel(x_prompt, x_sample, cache_k, cache_v, state_gla, page_table, c_prompt, c_sample, w_ada, b_ada, ln_g, ln_b, w_ffn1_in, w_ffn1_out, w_mix_in, w_gk_up, b_gk, gla_norm_g, w_mix_out, w_ffn2_in, w_ffn2_out):
    depth = w_ada.shape[0]
    assert depth == 1, "one decoder layer"
    d = x_prompt.shape[-1]
    alpha = (2.0 * depth) ** 0.25
    nb_p, s_p, _ = x_prompt.shape
    nb_s, t_s, _ = x_sample.shape
    past_len = page_table.shape[1] * PAGE_SIZE

    w1_in, w1_out = w_ffn1_in[0].astype(BF16), w_ffn1_out[0].astype(BF16)
    w2_in, w2_out = w_ffn2_in[0].astype(BF16), w_ffn2_out[0].astype(BF16)
    w_mix = w_mix_in[0]
    w_main = w_mix[:, :D_MIX_MAIN].astype(BF16)
    w_rg = jnp.pad(w_mix[:, D_MIX_MAIN:], ((0, 0), (0, LANES - GLA_GATE_RANK))).astype(BF16)
    w_gk = jnp.pad(w_gk_up[0], ((0, LANES - GLA_GATE_RANK), (0, 0)))
    w_mo = w_mix_out[0].astype(BF16)

    c_all = jnp.concatenate([c_sample, c_prompt], axis=0)
    m = _ada_modulation(c_all, w_ada[0], b_ada[0])
    m4 = m.reshape(3 * N_SUBLAYERS, c_all.shape[0], 1, d)

    def first_half(x, m_row0, pos_base, kv_transposed):
        x = _ffn(x, m4, m_row0, 0, w1_in, w1_out, ln_g[0, 0], ln_b[0, 0], alpha)
        return x, _mixer_in(x, m4, m_row0, w_main, w_rg, w_gk, b_gk[0], pos_base, kv_transposed)

    def second_half(x, m_row0, o_moba, o_gla):
        return _ffn(x, m4, m_row0, 2, w2_in, w2_out, ln_g[0, 2], ln_b[0, 2], alpha,
                    mixer_out=(o_moba, o_gla, w_mo, ln_g[0, 1], ln_b[0, 1]))

    k_pages = jnp.transpose(cache_k[0], (0, 2, 3, 1)).reshape(cache_k.shape[1], D_MOBA, PAGE_SIZE)
    v_pages = jnp.transpose(cache_v[0], (0, 2, 3, 1)).reshape(cache_v.shape[1], D_MOBA, PAGE_SIZE)

    xs, (qm_s, ks, vs, qg_s, kg_s, vg_s, gg_s, la_s) = first_half(x_sample, 0, past_len, False)
    xp, (qm_p, kp_t, vp_t, qg_p, kg_p, vg_p, gg_p, la_p) = first_half(x_prompt, nb_s, 0, True)
    o_moba_p, scores, gates = _moba_prompt(qm_p, kp_t, vp_t, qm_s, k_pages, page_table)
    o_gla_p, sp = _gla_prompt(qg_p, kg_p, la_p, vg_p, gg_p, gla_norm_g[0])
    yp = second_half(xp, nb_s, o_moba_p, o_gla_p)
    o_moba_s = _moba_sample(qm_s, ks, vs, scores, gates, v_pages, page_table)
    o_gla_s, ss = _gla_sample(qg_s, kg_s, la_s, vg_s, gg_s, state_gla[0], gla_norm_g[0])
    ys = second_half(xs, 0, o_moba_s, o_gla_s)

    def rows_major(a_t):
        a = a_t.reshape(nb_p, MOBA_HEADS, MOBA_HEAD_DIM, s_p)
        return jnp.transpose(a, (0, 3, 1, 2))[None]

    def heads(a):
        return a.reshape(1, nb_s, t_s, MOBA_HEADS, MOBA_HEAD_DIM)

    return (yp, ys, rows_major(kp_t), rows_major(vp_t), sp, heads(ks), heads(vs), ss)
```
